```python
import jax, jax.numpy as jnp
from jax import lax
import numpy as np

D_MODEL = 1024
BATCH = 32
SEQ = 2048
DEPTH = 2
DEC_BATCH = 32
DEC_SEQ = 32
PAST_LEN = 4096

CHUNK = 64
N_A_LAYERS = DEPTH // 2
N_B_LAYERS = DEPTH - N_A_LAYERS
PLE_DIM = 256
D_FF = 2816
NORM_EPS = 1e-6
SSM_EXPAND = 2
D_INNER = SSM_EXPAND * D_MODEL
SSM_HEAD_DIM = 64
SSM_HEADS = D_INNER // SSM_HEAD_DIM
SSM_GROUPS = 4
SSM_HEADS_PER_GROUP = SSM_HEADS // SSM_GROUPS
D_STATE = 128
CONV_W = 4
CONV_DIM = D_INNER + 2 * SSM_GROUPS * D_STATE
IN_DIM = D_INNER + CONV_DIM + SSM_HEADS
SSD_CHUNK = CHUNK
SB_HEAD_DIM = 64
SB_HEADS = D_MODEL // SB_HEAD_DIM
SB_KV_HEADS = 4
SB_Q_PER_KV = SB_HEADS // SB_KV_HEADS
Q_BLOCK = 128

kernel_name = "yoco_mamba2_stickbreak_stream_step"


def rms_norm(x, g):
    xf = x.astype(jnp.float32)
    y = xf * lax.rsqrt(jnp.mean(xf * xf, axis=-1, keepdims=True) + NORM_EPS)
    return (y * g.astype(jnp.float32)).astype(x.dtype)


def swiglu(h, w_gate, w_up, w_down):
    return (jax.nn.silu(h @ w_gate) * (h @ w_up)) @ w_down


def causal_dwconv(u, buf, w, b):
    L = u.shape[1]
    upad = jnp.concatenate([buf.astype(u.dtype), u], axis=1)
    out = b + upad[:, 0:L] * w[0]
    for k in range(1, CONV_W):
        out = out + upad[:, k:k + L] * w[k]
    return out, upad[:, -(CONV_W - 1):]


def ssd_scan(xdt, a, Bm, Cm, s0, chunk_len):
    b, L, G, E, P = xdt.shape
    nc = L // chunk_len

    def to_chunks(t):
        return jnp.moveaxis(t.reshape((b, nc, chunk_len) + t.shape[2:]), 1, 0)

    causal = jnp.tril(jnp.ones((chunk_len, chunk_len), dtype=bool))[None, :, :, None, None]

    def step(state, inp):
        x_c, a_c, B_c, C_c = inp
        a_cs = jnp.cumsum(a_c, axis=1)
        seg = a_cs[:, :, None] - a_cs[:, None, :]
        decay = jnp.exp(jnp.where(causal, seg, -jnp.inf))
        cb = jnp.einsum('bign,bjgn->bijg', C_c, B_c)
        y_diag = jnp.einsum('bijg,bijge,bjgep->bigep', cb, decay, x_c)
        y_off = jnp.einsum('bign,bgepn->bigep', C_c, state) * jnp.exp(a_cs)[..., None]
        w_end = jnp.exp(a_cs[:, -1:] - a_cs)
        new_state = state * jnp.exp(a_cs[:, -1])[..., None, None] + jnp.einsum(
            'bjgn,bjge,bjgep->bgepn', B_c, w_end, x_c)
        return new_state, y_diag + y_off

    s_fin, ys = lax.scan(step, s0, (to_chunks(xdt), to_chunks(a), to_chunks(Bm), to_chunks(Cm)))
    y = jnp.moveaxis(ys, 0, 1).reshape(b, L, G, E, P)
    return y, s_fin


def mamba2_mixer(h, ssm0, conv0, w_in, conv_w, conv_b, dt_bias, a_log, d_skip, norm_g, w_out):
    b, L, _ = h.shape
    zxbcdt = h @ w_in
    z = zxbcdt[..., :D_INNER]
    xbc = zxbcdt[..., D_INNER:D_INNER + CONV_DIM]
    dt_raw = zxbcdt[..., D_INNER + CONV_DIM:]
    xbc, conv_new = causal_dwconv(xbc, conv0, conv_w, conv_b)
    xbc = jax.nn.silu(xbc).astype(jnp.float32)
    xs = xbc[..., :D_INNER].reshape(b, L, SSM_GROUPS, SSM_HEADS_PER_GROUP, SSM_HEAD_DIM)
    Bm = xbc[..., D_INNER:D_INNER + SSM_GROUPS * D_STATE].reshape(b, L, SSM_GROUPS, D_STATE)
    Cm = xbc[..., D_INNER + SSM_GROUPS * D_STATE:].reshape(b, L, SSM_GROUPS, D_STATE)
    dt = jax.nn.softplus(dt_raw.astype(jnp.float32) + dt_bias.astype(jnp.float32))
    dt = dt.reshape(b, L, SSM_GROUPS, SSM_HEADS_PER_GROUP)
    A = -jnp.exp(a_log.astype(jnp.float32)).reshape(SSM_GROUPS, SSM_HEADS_PER_GROUP)
    s0 = ssm0.astype(jnp.float32).reshape(b, SSM_GROUPS, SSM_HEADS_PER_GROUP, SSM_HEAD_DIM, D_STATE)
    chunk_len = min(SSD_CHUNK, L)
    y, s_fin = ssd_scan(xs * dt[..., None], dt * A, Bm, Cm, s0, chunk_len)
    y = y + xs * d_skip.astype(jnp.float32).reshape(SSM_GROUPS, SSM_HEADS_PER_GROUP)[..., None]
    y = y.reshape(b, L, D_INNER) * jax.nn.silu(z.astype(jnp.float32))
    yg = y.reshape(b, L, SSM_GROUPS, D_INNER // SSM_GROUPS)
    yg = yg * lax.rsqrt(jnp.mean(yg * yg, axis=-1, keepdims=True) + NORM_EPS)
    y = yg.reshape(b, L, D_INNER) * norm_g.astype(jnp.float32)
    out = y.astype(h.dtype) @ w_out
    s_fin = s_fin.reshape(b, SSM_HEADS, SSM_HEAD_DIM, D_STATE).astype(ssm0.dtype)
    return out, s_fin, conv_new.astype(conv0.dtype)


def sb_block(q, k, v, q0):
    lq, lk = q.shape[1], k.shape[1]
    z = jnp.einsum('bqhgd,bkhd->bhgqk', q.astype(jnp.float32), k.astype(jnp.float32)) * (SB_HEAD_DIM ** -0.5)
    q_pos = q0 + jnp.arange(lq)
    mask = jnp.arange(lk)[None, :] < q_pos[:, None]
    log_1m_beta = jnp.where(mask, jax.nn.log_sigmoid(-z), 0.0)
    log_a = z + lax.cumsum(log_1m_beta, axis=4, reverse=True)
    a = jnp.exp(jnp.where(mask, log_a, -jnp.inf))
    out = jnp.einsum('bhgqk,bkhd->bqhgd', a, v.astype(jnp.float32))
    return out.astype(q.dtype)


def sb_attention(q, k, v, q_start):
    lq = q.shape[1]
    blk = min(Q_BLOCK, lq)
    outs = []
    for s0 in range(0, lq, blk):
        e = min(s0 + blk, lq)
        n_keys = q_start + e
        outs.append(sb_block(q[:, s0:e], k[:, :n_keys], v[:, :n_keys], q_start + s0))
    return jnp.concatenate(outs, axis=1)


def trunk(x, p, ssm0, conv0, k_past, v_past, q_start,
          ffn_norm, ffn_w_gate, ffn_w_up, ffn_w_down, mix_norm,
          ssm_w_in, ssm_conv_w, ssm_conv_b, ssm_dt_bias, ssm_a_log, ssm_d, ssm_norm, ssm_w_out,
          kv_norm, w_k, w_v, sb_w_q, sb_w_o,
          ple_norm, ple_w_gate, ple_w_proj, final_norm):
    b, L, _ = x.shape
    ssm_out, conv_out = [], []
    k_new = v_new = k_all = v_all = None
    for i in range(DEPTH):
        if i == N_A_LAYERS:
            kv_in = rms_norm(x, kv_norm)
            k_new = (kv_in @ w_k).reshape(b, L, SB_KV_HEADS, SB_HEAD_DIM)
            v_new = (kv_in @ w_v).reshape(b, L, SB_KV_HEADS, SB_HEAD_DIM)
            if k_past is None:
                k_all, v_all = k_new, v_new
            else:
                k_all = jnp.concatenate([k_past.astype(k_new.dtype), k_new], axis=1)
                v_all = jnp.concatenate([v_past.astype(v_new.dtype), v_new], axis=1)
        x = x + 0.5 * swiglu(rms_norm(x, ffn_norm[i, 0]), ffn_w_gate[i, 0], ffn_w_up[i, 0], ffn_w_down[i, 0])
        h = rms_norm(x, mix_norm[i])
        if i < N_A_LAYERS:
            mix, s_fin, c_fin = mamba2_mixer(h, ssm0[i], conv0[i], ssm_w_in[i], ssm_conv_w[i], ssm_conv_b[i],
                                             ssm_dt_bias[i], ssm_a_log[i], ssm_d[i], ssm_norm[i], ssm_w_out[i])
            ssm_out.append(s_fin)
            conv_out.append(c_fin)
        else:
            j = i - N_A_LAYERS
            q = (h @ sb_w_q[j]).reshape(b, L, SB_KV_HEADS, SB_Q_PER_KV, SB_HEAD_DIM)
            o = sb_attention(q, k_all, v_all, q_start)
            mix = o.reshape(b, L, SB_HEADS * SB_HEAD_DIM) @ sb_w_o[j]
        x = x + mix
        x = x + 0.5 * swiglu(rms_norm(x, ffn_norm[i, 1]), ffn_w_gate[i, 1], ffn_w_up[i, 1], ffn_w_down[i, 1])
        gate = jax.nn.sigmoid(rms_norm(x, ple_norm[i]) @ ple_w_gate[i])
        x = x + gate * (p[i] @ ple_w_proj[i])
    return rms_norm(x, final_norm), jnp.stack(ssm_out), jnp.stack(conv_out), k_new, v_new


def setup_inputs(seed: int = 0) -> dict:
    key = jax.random.key(seed)
    ks = jax.random.split(key, 40)

    def nrm(k, shape, scale):
        return jax.random.normal(k, shape, jnp.float32) * scale

    NA, NB = N_A_LAYERS, N_B_LAYERS
    dt0 = jnp.exp(jax.random.uniform(ks[20], (NA, SSM_HEADS), jnp.float32,
                                     minval=float(np.log(1e-3)), maxval=float(np.log(1e-1))))
    return {
        "x_prompt": nrm(ks[0], (BATCH, SEQ, D_MODEL), 1.0),
        "x_sample": nrm(ks[1], (DEC_BATCH, DEC_SEQ, D_MODEL), 1.0),
        "p_prompt": nrm(ks[2], (DEPTH, BATCH, SEQ, PLE_DIM), 1.0),
        "p_sample": nrm(ks[3], (DEPTH, DEC_BATCH, DEC_SEQ, PLE_DIM), 1.0),
        "state_ssm": nrm(ks[4], (NA, DEC_BATCH, SSM_HEADS, SSM_HEAD_DIM, D_STATE), 0.5),
        "state_conv": nrm(ks[5], (NA, DEC_BATCH, CONV_W - 1, CONV_DIM), 1.0),
        "cache_k": nrm(ks[6], (DEC_BATCH, PAST_LEN, SB_KV_HEADS, SB_HEAD_DIM), 1.0),
        "cache_v": nrm(ks[7], (DEC_BATCH, PAST_LEN, SB_KV_HEADS, SB_HEAD_DIM), 1.0),
        "ffn_norm": 1.0 + nrm(ks[8], (DEPTH, 2, D_MODEL), 0.05),
        "ffn_w_gate": nrm(ks[9], (DEPTH, 2, D_MODEL, D_FF), D_MODEL ** -0.5),
        "ffn_w_up": nrm(ks[10], (DEPTH, 2, D_MODEL, D_FF), D_MODEL ** -0.5),
        "ffn_w_down": nrm(ks[11], (DEPTH, 2, D_FF, D_MODEL), D_FF ** -0.5),
        "mix_norm": 1.0 + nrm(ks[12], (DEPTH, D_MODEL), 0.05),
        "ssm_w_in": nrm(ks[13], (NA, D_MODEL, IN_DIM), D_MODEL ** -0.5),
        "ssm_conv_w": nrm(ks[14], (NA, CONV_W, CONV_DIM), CONV_W ** -0.5),
        "ssm_conv_b": nrm(ks[15], (NA, CONV_DIM), 0.01),
        "ssm_dt_bias": dt0 + jnp.log(-jnp.expm1(-dt0)),
        "ssm_a_log": jnp.log(jax.random.uniform(ks[16], (NA, SSM_HEADS), jnp.float32, minval=1.0, maxval=16.0)),
        "ssm_d": 1.0 + nrm(ks[17], (NA, SSM_HEADS), 0.1),
        "ssm_norm": 1.0 + nrm(ks[18], (NA, D_INNER), 0.05),
        "ssm_w_out": nrm(ks[19], (NA, D_INNER, D_MODEL), D_INNER ** -0.5),
        "kv_norm": 1.0 + nrm(ks[21], (D_MODEL,), 0.05),
        "w_k": nrm(ks[22], (D_MODEL, SB_KV_HEADS * SB_HEAD_DIM), D_MODEL ** -0.5),
        "w_v": nrm(ks[23], (D_MODEL, SB_KV_HEADS * SB_HEAD_DIM), D_MODEL ** -0.5),
        "sb_w_q": nrm(ks[24], (NB, D_MODEL, SB_HEADS * SB_HEAD_DIM), D_MODEL ** -0.5),
        "sb_w_o": nrm(ks[25], (NB, SB_HEADS * SB_HEAD_DIM, D_MODEL), (SB_HEADS * SB_HEAD_DIM) ** -0.5),
        "ple_norm": 1.0 + nrm(ks[26], (DEPTH, D_MODEL), 0.05),
        "ple_w_gate": nrm(ks[27], (DEPTH, D_MODEL, D_MODEL), D_MODEL ** -0.5),
        "ple_w_proj": nrm(ks[28], (DEPTH, PLE_DIM, D_MODEL), PLE_DIM ** -0.5),
        "final_norm": 1.0 + nrm(ks[29], (D_MODEL,), 0.05),
    }


def reference(x_prompt, x_sample, p_prompt, p_sample, state_ssm, state_conv, cache_k, cache_v,
              ffn_norm, ffn_w_gate, ffn_w_up, ffn_w_down, mix_norm,
              ssm_w_in, ssm_conv_w, ssm_conv_b, ssm_dt_bias, ssm_a_log, ssm_d, ssm_norm, ssm_w_out,
              kv_norm, w_k, w_v, sb_w_q, sb_w_o,
              ple_norm, ple_w_gate, ple_w_proj, final_norm):
    weights = (ffn_norm, ffn_w_gate, ffn_w_up, ffn_w_down, mix_norm,
               ssm_w_in, ssm_conv_w, ssm_conv_b, ssm_dt_bias, ssm_a_log, ssm_d, ssm_norm, ssm_w_out,
               kv_norm, w_k, w_v, sb_w_q, sb_w_o,
               ple_norm, ple_w_gate, ple_w_proj, final_norm)
    bp = x_prompt.shape[0]
    ssm0_p = jnp.zeros((N_A_LAYERS, bp, SSM_HEADS, SSM_HEAD_DIM, D_STATE), state_ssm.dtype)
    conv0_p = jnp.zeros((N_A_LAYERS, bp, CONV_W - 1, CONV_DIM), state_conv.dtype)
    y_prompt, ssm_p, conv_p, k_p, v_p = trunk(x_prompt, p_prompt, ssm0_p, conv0_p, None, None, 0, *weights)
    y_sample, ssm_s, conv_s, k_s, v_s = trunk(x_sample, p_sample, state_ssm, state_conv, cache_k, cache_v,
                                              cache_k.shape[1], *weights)
    return (y_prompt, y_sample, ssm_p, conv_p, k_p, v_p, ssm_s, conv_s, k_s, v_s)
```

```python
import functools

import jax
import jax.numpy as jnp
from jax import lax
from jax.experimental import pallas as pl
from jax.experimental.pallas import tpu as pltpu

F32 = jnp.float32
BF16 = jnp.bfloat16
NORM_EPS = 1e-6

SSM_HEAD_DIM = 64
SSM_GROUPS = 4
D_STATE = 128
CONV_W = 4
SSD_CHUNK = 64
SB_HEAD_DIM = 64
SB_KV_HEADS = 4
Q_BLOCK = 128
KEY_BLOCK = 128

VMEM_LIMIT_BYTES = 56 * 1024 * 1024
TOKEN_TILE = 512
MAMBA_TILE = 256
FFN_CHUNKS = 2
CONV_PAD = 8


def _rms(x, g):
    return x * lax.rsqrt(jnp.mean(x * x, axis=-1, keepdims=True) + NORM_EPS) * g


def _sigmoid(v):
    return 1.0 / (1.0 + jnp.exp(-v))


def _silu(v):
    return v * _sigmoid(v)


def _softplus(v):
    return jnp.maximum(v, 0.0) + jnp.log(1.0 + jnp.exp(-jnp.abs(v)))


def _dot(a, b):
    return jnp.dot(a, b, preferred_element_type=F32)


def _dot_nt(a, b):
    return lax.dot_general(a, b, (((1,), (1,)), ((), ())), preferred_element_type=F32)


def _dot_tn(a, b):
    return lax.dot_general(a, b, (((0,), (0,)), ((), ())), preferred_element_type=F32)


def _dot_exact(a, b):
    return jnp.dot(a, b, precision=lax.Precision.HIGHEST, preferred_element_type=F32)


def _resident(shape):
    zeros = (0,) * len(shape)
    return pl.BlockSpec(shape, lambda *_: zeros, pipeline_mode=pl.Buffered(1))


def _params(n_axes):
    return pltpu.CompilerParams(dimension_semantics=("arbitrary",) * n_axes,
                                vmem_limit_bytes=VMEM_LIMIT_BYTES)


def _ffn_body(x_ref, g_ref, wg_ref, wu_ref, wd_ref, o_ref):
    x = x_ref[...]
    h = _rms(x, g_ref[...]).astype(BF16)
    d_ff = wg_ref.shape[1]
    fc = d_ff // FFN_CHUNKS
    acc = jnp.zeros_like(x)
    for c in range(FFN_CHUNKS):
        gate = _dot(h, wg_ref[:, c * fc:(c + 1) * fc])
        up = _dot(h, wu_ref[:, c * fc:(c + 1) * fc])
        act = (_silu(gate) * up).astype(BF16)
        acc = acc + _dot(act, wd_ref[c * fc:(c + 1) * fc, :])
    o_ref[...] = x + 0.5 * acc


def _ffn(x, g, wg, wu, wd):
    t, d = x.shape
    d_ff = wg.shape[1]
    tm = min(TOKEN_TILE, t)
    return pl.pallas_call(
        _ffn_body,
        grid=(t // tm,),
        in_specs=[pl.BlockSpec((tm, d), lambda i: (i, 0)),
                  _resident((1, d)), _resident((d, d_ff)), _resident((d, d_ff)),
                  _resident((d_ff, d))],
        out_specs=pl.BlockSpec((tm, d), lambda i: (i, 0)),
        out_shape=jax.ShapeDtypeStruct((t, d), F32),
        compiler_params=_params(1),
        name="ffn",
    )(x, g, wg, wu, wd)


def _ple_body(x_ref, p_ref, g_ref, wgate_ref, wproj_ref, fin_ref, o_ref, *, final):
    x = x_ref[...]
    h = _rms(x, g_ref[...]).astype(BF16)
    gate = _sigmoid(_dot(h, wgate_ref[...]))
    proj = _dot(p_ref[...].astype(BF16), wproj_ref[...])
    y = x + gate * proj
    if final:
        y = _rms(y, fin_ref[...])
    o_ref[...] = y


def _ple(x, p, g, wgate, wproj, fin_g, final):
    t, d = x.shape
    pd = p.shape[1]
    tm = min(TOKEN_TILE, t)
    return pl.pallas_call(
        functools.partial(_ple_body, final=final),
        grid=(t // tm,),
        in_specs=[pl.BlockSpec((tm, d), lambda i: (i, 0)),
                  pl.BlockSpec((tm, pd), lambda i: (i, 0)),
                  _resident((1, d)), _resident((d, d)), _resident((pd, d)), _resident((1, d))],
        out_specs=pl.BlockSpec((tm, d), lambda i: (i, 0)),
        out_shape=jax.ShapeDtypeStruct((t, d), F32),
        compiler_params=_params(1),
        name="ple",
    )(x, p, g, wgate, wproj, fin_g)


def _kv_body(x_ref, g_ref, wk_ref, wv_ref, k_ref, v_ref, kh_ref, vh_ref):
    h = _rms(x_ref[0], g_ref[...]).astype(BF16)
    k = _dot(h, wk_ref[...])
    v = _dot(h, wv_ref[...])
    k_ref[0] = k
    v_ref[0] = v
    for j in range(SB_KV_HEADS):
        kh_ref[0, j] = k[:, j * SB_HEAD_DIM:(j + 1) * SB_HEAD_DIM].astype(BF16)
        vh_ref[0, j] = v[:, j * SB_HEAD_DIM:(j + 1) * SB_HEAD_DIM].astype(BF16)


def _kv(x, g, wk, wv):
    b, l, d = x.shape
    dkv = wk.shape[1]
    tm = min(TOKEN_TILE, l)
    row = pl.BlockSpec((1, tm, dkv), lambda i, j: (i, j, 0))
    head = pl.BlockSpec((1, SB_KV_HEADS, tm, SB_HEAD_DIM), lambda i, j: (i, 0, j, 0))
    return pl.pallas_call(
        _kv_body,
        grid=(b, l // tm),
        in_specs=[pl.BlockSpec((1, tm, d), lambda i, j: (i, j, 0)),
                  _resident((1, d)), _resident((d, dkv)), _resident((d, dkv))],
        out_specs=[row, row, head, head],
        out_shape=[jax.ShapeDtypeStruct((b, l, dkv), F32),
                   jax.ShapeDtypeStruct((b, l, dkv), F32),
                   jax.ShapeDtypeStruct((b, SB_KV_HEADS, l, SB_HEAD_DIM), BF16),
                   jax.ShapeDtypeStruct((b, SB_KV_HEADS, l, SB_HEAD_DIM), BF16)],
        compiler_params=_params(2),
        name="kv_proj",
    )(x, g, wk, wv)


def _attn_body(x_ref, g_ref, wq_ref, wo_ref, k_ref, v_ref, o_ref, *, blocks_before):
    qb = x_ref.shape[1]
    q_per_kv = wq_ref.shape[1] // (SB_KV_HEADS * SB_HEAD_DIM)
    rows = q_per_kv * qb
    n_before = pl.program_id(1) if blocks_before is None else blocks_before

    x = x_ref[0]
    h = _rms(x, g_ref[...]).astype(BF16)
    q = (_dot(h, wq_ref[...]) * (SB_HEAD_DIM ** -0.5)).astype(BF16)

    key_i = lax.broadcasted_iota(jnp.int32, (KEY_BLOCK, KEY_BLOCK), 0)
    key_s = lax.broadcasted_iota(jnp.int32, (KEY_BLOCK, KEY_BLOCK), 1)
    suffix_ones = (key_i >= key_s).astype(BF16)
    row_q = lax.broadcasted_iota(jnp.int32, (rows, KEY_BLOCK), 0) % qb
    col_k = lax.broadcasted_iota(jnp.int32, (rows, KEY_BLOCK), 1)
    causal = col_k < row_q

    def sweep(j, qs, start, carry, acc, masked):
        kblk = k_ref[0, j, pl.ds(start, KEY_BLOCK), :]
        vblk = v_ref[0, j, pl.ds(start, KEY_BLOCK), :]
        z = _dot_nt(qs, kblk)
        c = -_softplus(z)
        if masked:
            c = jnp.where(causal, c, 0.0)
        c_hi = c.astype(BF16)
        c_lo = (c - c_hi.astype(F32)).astype(BF16)
        cs = _dot(c_hi, suffix_ones) + _dot(c_lo, suffix_ones)
        log_a = z + cs + carry
        if masked:
            log_a = jnp.where(causal, log_a, -jnp.inf)
        a = jnp.exp(log_a).astype(BF16)
        acc = acc + _dot(a, vblk)
        carry = carry + cs[:, 0:1]
        return carry, acc

    head_out = []
    for j in range(SB_KV_HEADS):
        qs = jnp.concatenate(
            [q[:, (j * q_per_kv + g) * SB_HEAD_DIM:(j * q_per_kv + g + 1) * SB_HEAD_DIM]
             for g in range(q_per_kv)], axis=0)
        carry = jnp.zeros((rows, 1), F32)
        acc = jnp.zeros((rows, SB_HEAD_DIM), F32)
        diag_start = n_before * KEY_BLOCK
        if blocks_before is None:
            diag_start = pl.multiple_of(diag_start, KEY_BLOCK)
        carry, acc = sweep(j, qs, diag_start, carry, acc, True)

        def body(i, state, j=j, qs=qs):
            start = pl.multiple_of((n_before - 1 - i) * KEY_BLOCK, KEY_BLOCK)
            return sweep(j, qs, start, state[0], state[1], False)

        carry, acc = lax.fori_loop(0, n_before, body, (carry, acc))
        for g in range(q_per_kv):
            head_out.append(acc[g * qb:(g + 1) * qb, :])
    o = jnp.concatenate(head_out, axis=1).astype(BF16)
    o_ref[0] = x + _dot(o, wo_ref[...])


def _attn(x, g, wq, wo, kh, vh, blocks_before):
    b, l, d = x.shape
    dq = wq.shape[1]
    nk = kh.shape[2]
    qb = min(Q_BLOCK, l)
    kv_spec = pl.BlockSpec((1, SB_KV_HEADS, nk, SB_HEAD_DIM), lambda i, j: (i, 0, 0, 0))
    return pl.pallas_call(
        functools.partial(_attn_body, blocks_before=blocks_before),
        grid=(b, l // qb),
        in_specs=[pl.BlockSpec((1, qb, d), lambda i, j: (i, j, 0)),
                  _resident((1, d)), _resident((d, dq)), _resident((dq, d)),
                  kv_spec, kv_spec],
        out_specs=pl.BlockSpec((1, qb, d), lambda i, j: (i, j, 0)),
        out_shape=jax.ShapeDtypeStruct((b, l, d), F32),
        compiler_params=_params(2),
        name="sb_attn",
    )(x, g, wq, wo, kh, vh)


def _mamba_body(x_ref, g_ref, wz_ref, wx_ref, wdt_ref, wdtt_ref, cw_ref, cb_ref,
                dtb_ref, dtbt_ref, alog_ref, alogt_ref, dskip_ref, ng_ref, wout_ref,
                s0_ref, c0_ref,
                o_ref, sfin_ref, cfin_ref,
                state_ref, pad_ref, y_ref, *, chunk, has_init):
    tt = x_ref.shape[1]
    d_inner = wz_ref.shape[1]
    n_heads = wdt_ref.shape[1]
    hpg = n_heads // SSM_GROUPS
    gw = hpg * SSM_HEAD_DIM
    gn = SSM_GROUPS * D_STATE
    t = pl.program_id(1)

    @pl.when(t == 0)
    def _init():
        pad_ref[0:CONV_PAD, :] = jnp.zeros((CONV_PAD, pad_ref.shape[1]), F32)
        if has_init:
            for g in range(SSM_GROUPS):
                state_ref[g] = s0_ref[0, g].T
            pad_ref[CONV_PAD - (CONV_W - 1):CONV_PAD, :] = c0_ref[0]
        else:
            state_ref[...] = jnp.zeros(state_ref.shape, F32)

    x = x_ref[0]
    h = _rms(x, g_ref[...]).astype(BF16)
    z = _dot(h, wz_ref[...])
    u = _dot(h, wx_ref[...])
    pad_ref[CONV_PAD:CONV_PAD + tt, :] = u
    conv = cb_ref[...]
    for k in range(CONV_W):
        off = CONV_PAD - (CONV_W - 1) + k
        conv = conv + pad_ref[off:off + tt, :] * cw_ref[k:k + 1, :]
    xbc = _silu(conv)
    xs = xbc[:, :d_inner]
    bm = xbc[:, d_inner:d_inner + gn].astype(BF16)
    cm = xbc[:, d_inner + gn:].astype(BF16)

    dt = _softplus(_dot(h, wdt_ref[...]) + dtb_ref[...])
    dtt = _softplus(_dot_nt(wdtt_ref[...], h) + dtbt_ref[...])
    a = dt * (-jnp.exp(alog_ref[...]))
    at = dtt * (-jnp.exp(alogt_ref[...]))

    ci = lax.broadcasted_iota(jnp.int32, (chunk, chunk), 0)
    cj = lax.broadcasted_iota(jnp.int32, (chunk, chunk), 1)
    lower = ci >= cj
    lower_f = lower.astype(F32)
    upper_f = (ci <= cj).astype(F32)

    for c in range(tt // chunk):
        r0 = c * chunk
        a_c = a[r0:r0 + chunk, :]
        acs = _dot_exact(lower_f, a_c)
        acst = _dot_exact(at[:, r0:r0 + chunk], upper_f)
        e_acs = jnp.exp(acs)
        acs_last = acs[chunk - 1:chunk, :]
        dw = dt[r0:r0 + chunk, :] * jnp.exp(acs_last - acs)
        e_last = jnp.exp(acs_last)
        dtt_c = dtt[:, r0:r0 + chunk]
        for g in range(SSM_GROUPS):
            b_c = bm[r0:r0 + chunk, g * D_STATE:(g + 1) * D_STATE]
            c_c = cm[r0:r0 + chunk, g * D_STATE:(g + 1) * D_STATE]
            cb = _dot_nt(c_c, b_c)
            s_g = state_ref[g]
            y_off = _dot(c_c, s_g.astype(BF16))
            xw = []
            dec = []
            for e in range(hpg):
                hd = g * hpg + e
                lo = hd * SSM_HEAD_DIM
                x_h = xs[r0:r0 + chunk, lo:lo + SSM_HEAD_DIM]
                seg = acs[:, hd:hd + 1] - acst[hd:hd + 1, :]
                decay = jnp.exp(jnp.where(lower, seg, -jnp.inf))
                m = (cb * decay * dtt_c[hd:hd + 1, :]).astype(BF16)
                y_h = _dot(m, x_h.astype(BF16))
                y_h = y_h + y_off[:, e * SSM_HEAD_DIM:(e + 1) * SSM_HEAD_DIM] * e_acs[:, hd:hd + 1]
                y_ref[r0:r0 + chunk, lo:lo + SSM_HEAD_DIM] = y_h
                xw.append(x_h * dw[:, hd:hd + 1])
                dec.append(jnp.broadcast_to(e_last[:, hd:hd + 1], (1, SSM_HEAD_DIM)))
            xw_g = jnp.concatenate(xw, axis=1).astype(BF16)
            dec_g = jnp.concatenate(dec, axis=1)
            state_ref[g] = s_g * dec_g + _dot_tn(b_c, xw_g)

    y = y_ref[...] + xs * dskip_ref[...]
    y = y * _silu(z)
    normed = []
    for g in range(SSM_GROUPS):
        yg = y[:, g * gw:(g + 1) * gw]
        normed.append(yg * lax.rsqrt(jnp.mean(yg * yg, axis=-1, keepdims=True) + NORM_EPS))
    y = jnp.concatenate(normed, axis=1) * ng_ref[...]
    o_ref[0] = x + _dot(y.astype(BF16), wout_ref[...])

    @pl.when(t == pl.num_programs(1) - 1)
    def _final():
        for g in range(SSM_GROUPS):
            sfin_ref[0, g] = state_ref[g].T
        cfin_ref[0] = pad_ref[CONV_PAD + tt - (CONV_W - 1):CONV_PAD + tt, :]

    pad_ref[0:CONV_PAD, :] = pad_ref[tt:tt + CONV_PAD, :]


def _mamba(x, g, w, s0, c0):
    b, l, d = x.shape
    d_inner = w["wz"].shape[1]
    conv_dim = w["wx"].shape[1]
    n_heads = w["wdt"].shape[1]
    gw = (n_heads // SSM_GROUPS) * SSM_HEAD_DIM
    tt = min(MAMBA_TILE, l)
    chunk = min(SSD_CHUNK, l)
    has_init = s0 is not None
    if not has_init:
        s0 = jnp.zeros((1, SSM_GROUPS, gw, D_STATE), F32)
        c0 = jnp.zeros((1, CONV_W - 1, conv_dim), F32)
        init_map = lambda i, j: (0, 0, 0, 0)
        conv_map = lambda i, j: (0, 0, 0)
    else:
        init_map = lambda i, j: (i, 0, 0, 0)
        conv_map = lambda i, j: (i, 0, 0)
    state_spec = pl.BlockSpec((1, SSM_GROUPS, gw, D_STATE), lambda i, j: (i, 0, 0, 0))
    conv_spec = pl.BlockSpec((1, CONV_W - 1, conv_dim), lambda i, j: (i, 0, 0))
    return pl.pallas_call(
        functools.partial(_mamba_body, chunk=chunk, has_init=has_init),
        grid=(b, l // tt),
        in_specs=[pl.BlockSpec((1, tt, d), lambda i, j: (i, j, 0)),
                  _resident((1, d)), _resident((d, d_inner)), _resident((d, conv_dim)),
                  _resident((d, n_heads)), _resident((n_heads, d)),
                  _resident((CONV_W, conv_dim)), _resident((1, conv_dim)),
                  _resident((1, n_heads)), _resident((n_heads, 1)),
                  _resident((1, n_heads)), _resident((n_heads, 1)),
                  _resident((1, d_inner)), _resident((1, d_inner)), _resident((d_inner, d)),
                  pl.BlockSpec((1, SSM_GROUPS, gw, D_STATE), init_map),
                  pl.BlockSpec((1, CONV_W - 1, conv_dim), conv_map)],
        out_specs=[pl.BlockSpec((1, tt, d), lambda i, j: (i, j, 0)), state_spec, conv_spec],
        out_shape=[jax.ShapeDtypeStruct((b, l, d), F32),
                   jax.ShapeDtypeStruct((b, SSM_GROUPS, gw, D_STATE), F32),
                   jax.ShapeDtypeStruct((b, CONV_W - 1, conv_dim), F32)],
        scratch_shapes=[pltpu.VMEM((SSM_GROUPS, D_STATE, gw), F32),
                        pltpu.VMEM((CONV_PAD + tt, conv_dim), F32),
                        pltpu.VMEM((tt, d_inner), F32)],
        compiler_params=_params(2),
        name="mamba2",
    )(x, g, w["wz"], w["wx"], w["wdt"], w["wdtt"], w["conv_w"], w["conv_b"],
      w["dt_bias"], w["dt_bias_t"], w["a_log"], w["a_log_t"], w["d_skip"], w["norm_g"],
      w["wout"], s0, c0)


def _prepare_mamba(w_in, conv_w, conv_b, dt_bias, a_log, d_skip, norm_g, w_out):
    d_inner = w_out.shape[0]
    n_heads = dt_bias.shape[0]
    conv_dim = conv_w.shape[1]
    wdt = w_in[:, d_inner + conv_dim:].astype(BF16)
    return {
        "wz": w_in[:, :d_inner].astype(BF16),
        "wx": w_in[:, d_inner:d_inner + conv_dim].astype(BF16),
        "wdt": wdt,
        "wdtt": wdt.T,
        "conv_w": conv_w,
        "conv_b": conv_b.reshape(1, conv_dim),
        "dt_bias": dt_bias.reshape(1, n_heads),
        "dt_bias_t": dt_bias.reshape(n_heads, 1),
        "a_log": a_log.reshape(1, n_heads),
        "a_log_t": a_log.reshape(n_heads, 1),
        "d_skip": jnp.repeat(d_skip, SSM_HEAD_DIM).reshape(1, d_inner),
        "norm_g": norm_g.reshape(1, d_inner),
        "wout": w_out.astype(BF16),
    }


def _trunk(x, p, ssm0, conv0, k_past, v_past, wts):
    b, l, d = x.shape
    t = b * l
    depth = wts["ffn_norm"].shape[0]
    n_a = len(wts["mamba"])
    ssm_out, conv_out = [], []
    k_new = v_new = kh = vh = None
    blocks_before = None
    xf = x.reshape(t, d)
    for i in range(depth):
        if i == n_a:
            k_new, v_new, kh, vh = _kv(xf.reshape(b, l, d), wts["kv_norm"], wts["w_k"], wts["w_v"])
            if k_past is not None:
                past = k_past.shape[1]
                blocks_before = past // KEY_BLOCK
                pad = (-(past + l)) % KEY_BLOCK

                def with_past(past_rows, new_heads):
                    ph = jnp.transpose(past_rows, (0, 2, 1, 3)).astype(BF16)
                    tail = jnp.zeros(new_heads.shape[:2] + (pad, SB_HEAD_DIM), BF16)
                    return jnp.concatenate([ph, new_heads, tail], axis=2)

                kh = with_past(k_past, kh)
                vh = with_past(v_past, vh)
        f = wts["ffn"][i][0]
        xf = _ffn(xf, f["g"], f["wg"], f["wu"], f["wd"])
        if i < n_a:
            s0 = None if ssm0 is None else ssm0[i].reshape(b, SSM_GROUPS, -1, D_STATE)
            c0 = None if conv0 is None else conv0[i]
            x3, s_fin, c_fin = _mamba(xf.reshape(b, l, d), wts["mix_norm"][i], wts["mamba"][i], s0, c0)
            xf = x3.reshape(t, d)
            ssm_out.append(s_fin.reshape(b, -1, SSM_HEAD_DIM, D_STATE))
            conv_out.append(c_fin)
        else:
            jb = i - n_a
            x3 = _attn(xf.reshape(b, l, d), wts["mix_norm"][i], wts["wq"][jb], wts["wo"][jb],
                       kh, vh, blocks_before)
            xf = x3.reshape(t, d)
        f = wts["ffn"][i][1]
        xf = _ffn(xf, f["g"], f["wg"], f["wu"], f["wd"])
        e = wts["ple"][i]
        xf = _ple(xf, p[i].reshape(t, -1), e["g"], e["wgate"], e["wproj"], wts["final_norm"],
                  final=(i == depth - 1))
    kv_shape = (b, l, SB_KV_HEADS, SB_HEAD_DIM)
    return (xf.reshape(b, l, d), jnp.stack(ssm_out), jnp.stack(conv_out),
            k_new.reshape(kv_shape), v_new.reshape(kv_shape))


def kernel(x_prompt, x_sample, p_prompt, p_sample, state_ssm, state_conv, cache_k, cache_v,
           ffn_norm, ffn_w_gate, ffn_w_up, ffn_w_down, mix_norm,
           ssm_w_in, ssm_conv_w, ssm_conv_b, ssm_dt_bias, ssm_a_log, ssm_d, ssm_norm, ssm_w_out,
           kv_norm, w_k, w_v, sb_w_q, sb_w_o,
           ple_norm, ple_w_gate, ple_w_proj, final_norm):
    depth, d = mix_norm.shape
    n_a = ssm_w_in.shape[0]
    n_b = sb_w_q.shape[0]
    wts = {
        "ffn_norm": ffn_norm,
        "ffn": [[{"g": ffn_norm[i, s].reshape(1, d),
                  "wg": ffn_w_gate[i, s].astype(BF16),
                  "wu": ffn_w_up[i, s].astype(BF16),
                  "wd": ffn_w_down[i, s].astype(BF16)} for s in range(2)] for i in range(depth)],
        "mix_norm": [mix_norm[i].reshape(1, d) for i in range(depth)],
        "mamba": [_prepare_mamba(ssm_w_in[i], ssm_conv_w[i], ssm_conv_b[i], ssm_dt_bias[i],
                                 ssm_a_log[i], ssm_d[i], ssm_norm[i], ssm_w_out[i])
                  for i in range(n_a)],
        "kv_norm": kv_norm.reshape(1, d),
        "w_k": w_k.astype(BF16),
        "w_v": w_v.astype(BF16),
        "wq": [sb_w_q[j].astype(BF16) for j in range(n_b)],
        "wo": [sb_w_o[j].astype(BF16) for j in range(n_b)],
        "ple": [{"g": ple_norm[i].reshape(1, d),
                 "wgate": ple_w_gate[i].astype(BF16),
                 "wproj": ple_w_proj[i].astype(BF16)} for i in range(depth)],
        "final_norm": final_norm.reshape(1, d),
    }
    y_p, ssm_p, conv_p, k_p, v_p = _trunk(x_prompt, p_prompt, None, None, None, None, wts)
    y_s, ssm_s, conv_s, k_s, v_s = _trunk(x_sample, p_sample, state_ssm, state_conv,
                                          cache_k, cache_v, wts)
    return (y_p, y_s, ssm_p, conv_p, k_p, v_p, ssm_s, conv_s, k_s, v_s)
```

```python
import functools

import jax
import jax.numpy as jnp
from jax import lax
from jax.experimental import pallas as pl
from jax.experimental.pallas import tpu as pltpu

F32 = jnp.float32
BF16 = jnp.bfloat16
NORM_EPS = 1e-6

SSM_HEAD_DIM = 64
SSM_GROUPS = 4
D_STATE = 128
CONV_W = 4
SSD_CHUNK = 64
SB_HEAD_DIM = 64
SB_KV_HEADS = 4
Q_BLOCK = 128
KEY_BLOCK = 128

VMEM_LIMIT_BYTES = 56 * 1024 * 1024
TOKEN_TILE = 512
MAMBA_TILE = 256
FFN_CHUNKS = 2
CONV_PAD = 8


def _rms(x, g):
    return x * lax.rsqrt(jnp.mean(x * x, axis=-1, keepdims=True) + NORM_EPS) * g


def _sigmoid(v):
    return 1.0 / (1.0 + jnp.exp(-v))


def _silu(v):
    return v * _sigmoid(v)


def _softplus(v):
    return jnp.maximum(v, 0.0) + jnp.log(1.0 + jnp.exp(-jnp.abs(v)))


def _dot(a, b):
    return jnp.dot(a, b, preferred_element_type=F32)


def _dot_nt(a, b):
    return lax.dot_general(a, b, (((1,), (1,)), ((), ())), preferred_element_type=F32)


def _dot_tn(a, b):
    return lax.dot_general(a, b, (((0,), (0,)), ((), ())), preferred_element_type=F32)


def _dot_exact(a, b):
    return jnp.dot(a, b, precision=lax.Precision.HIGHEST, preferred_element_type=F32)


def _resident(shape):
    zeros = (0,) * len(shape)
    return pl.BlockSpec(shape, lambda *_: zeros, pipeline_mode=pl.Buffered(1))


def _params(n_axes):
    return pltpu.CompilerParams(dimension_semantics=("arbitrary",) * n_axes,
                                vmem_limit_bytes=VMEM_LIMIT_BYTES)


def _ffn_body(x_ref, g_ref, wg_ref, wu_ref, wd_ref, o_ref):
    x = x_ref[...]
    h = _rms(x, g_ref[...]).astype(BF16)
    d_ff = wg_ref.shape[1]
    fc = d_ff // FFN_CHUNKS
    acc = jnp.zeros_like(x)
    for c in range(FFN_CHUNKS):
        gate = _dot(h, wg_ref[:, c * fc:(c + 1) * fc])
        up = _dot(h, wu_ref[:, c * fc:(c + 1) * fc])
        act = (_silu(gate) * up).astype(BF16)
        acc = acc + _dot(act, wd_ref[c * fc:(c + 1) * fc, :])
    o_ref[...] = x + 0.5 * acc


def _ffn(x, g, wg, wu, wd):
    t, d = x.shape
    d_ff = wg.shape[1]
    tm = min(TOKEN_TILE, t)
    return pl.pallas_call(
        _ffn_body,
        grid=(t // tm,),
        in_specs=[pl.BlockSpec((tm, d), lambda i: (i, 0)),
                  _resident((1, d)), _resident((d, d_ff)), _resident((d, d_ff)),
                  _resident((d_ff, d))],
        out_specs=pl.BlockSpec((tm, d), lambda i: (i, 0)),
        out_shape=jax.ShapeDtypeStruct((t, d), F32),
        compiler_params=_params(1),
        name="ffn",
    )(x, g, wg, wu, wd)


def _ple_body(x_ref, p_ref, g_ref, wgate_ref, wproj_ref, fin_ref, o_ref, *, final):
    x = x_ref[...]
    h = _rms(x, g_ref[...]).astype(BF16)
    gate = _sigmoid(_dot(h, wgate_ref[...]))
    proj = _dot(p_ref[...].astype(BF16), wproj_ref[...])
    y = x + gate * proj
    if final:
        y = _rms(y, fin_ref[...])
    o_ref[...] = y


def _ple(x, p, g, wgate, wproj, fin_g, final):
    t, d = x.shape
    pd = p.shape[1]
    tm = min(TOKEN_TILE, t)
    return pl.pallas_call(
        functools.partial(_ple_body, final=final),
        grid=(t // tm,),
        in_specs=[pl.BlockSpec((tm, d), lambda i: (i, 0)),
                  pl.BlockSpec((tm, pd), lambda i: (i, 0)),
                  _resident((1, d)), _resident((d, d)), _resident((pd, d)), _resident((1, d))],
        out_specs=pl.BlockSpec((tm, d), lambda i: (i, 0)),
        out_shape=jax.ShapeDtypeStruct((t, d), F32),
        compiler_params=_params(1),
        name="ple",
    )(x, p, g, wgate, wproj, fin_g)


def _kv_body(x_ref, g_ref, wk_ref, wv_ref, k_ref, v_ref):
    h = _rms(x_ref[...], g_ref[...]).astype(BF16)
    k_ref[...] = _dot(h, wk_ref[...])
    v_ref[...] = _dot(h, wv_ref[...])


def _kv(x, g, wk, wv):
    t, d = x.shape
    dkv = wk.shape[1]
    tm = min(TOKEN_TILE, t)
    row = pl.BlockSpec((tm, dkv), lambda i: (i, 0))
    return pl.pallas_call(
        _kv_body,
        grid=(t // tm,),
        in_specs=[pl.BlockSpec((tm, d), lambda i: (i, 0)),
                  _resident((1, d)), _resident((d, dkv)), _resident((d, dkv))],
        out_specs=[row, row],
        out_shape=[jax.ShapeDtypeStruct((t, dkv), F32), jax.ShapeDtypeStruct((t, dkv), F32)],
        compiler_params=_params(1),
        name="kv_proj",
    )(x, g, wk, wv)


def _kv_layout_body(k_ref, v_ref, kt_ref, vb_ref):
    tm = k_ref.shape[1]
    hd = SB_HEAD_DIM
    pw = 2 * hd
    kt = k_ref[0].T.astype(BF16)
    v = v_ref[0].astype(BF16)
    first_head = lax.broadcasted_iota(jnp.int32, (KEY_BLOCK, pw), 1) < hd
    zeros_k = jnp.zeros((hd, KEY_BLOCK), BF16)
    zeros_v = jnp.zeros((KEY_BLOCK, pw), BF16)
    for pr in range(SB_KV_HEADS // 2):
        for kb in range(tm // KEY_BLOCK):
            keys = slice(kb * KEY_BLOCK, (kb + 1) * KEY_BLOCK)
            kt_ref[0, pr, kb, 0:hd, 0:KEY_BLOCK] = kt[(2 * pr) * hd:(2 * pr + 1) * hd, keys]
            kt_ref[0, pr, kb, 0:hd, KEY_BLOCK:2 * KEY_BLOCK] = zeros_k
            kt_ref[0, pr, kb, hd:pw, 0:KEY_BLOCK] = zeros_k
            kt_ref[0, pr, kb, hd:pw, KEY_BLOCK:2 * KEY_BLOCK] = kt[(2 * pr + 1) * hd:(2 * pr + 2) * hd, keys]
            vp = v[keys, pr * pw:(pr + 1) * pw]
            vb_ref[0, pr, kb, 0:KEY_BLOCK, :] = jnp.where(first_head, vp, zeros_v)
            vb_ref[0, pr, kb, KEY_BLOCK:2 * KEY_BLOCK, :] = jnp.where(first_head, zeros_v, vp)


def _kv_layout(k, v):
    b, nk, dkv = k.shape
    tm = min(TOKEN_TILE, nk)
    n_pairs = SB_KV_HEADS // 2
    pw = 2 * SB_HEAD_DIM
    nb = tm // KEY_BLOCK
    return pl.pallas_call(
        _kv_layout_body,
        grid=(b, nk // tm),
        in_specs=[pl.BlockSpec((1, tm, dkv), lambda i, j: (i, j, 0)),
                  pl.BlockSpec((1, tm, dkv), lambda i, j: (i, j, 0))],
        out_specs=[pl.BlockSpec((1, n_pairs, nb, pw, 2 * KEY_BLOCK), lambda i, j: (i, 0, j, 0, 0)),
                   pl.BlockSpec((1, n_pairs, nb, 2 * KEY_BLOCK, pw), lambda i, j: (i, 0, j, 0, 0))],
        out_shape=[jax.ShapeDtypeStruct((b, n_pairs, nk // KEY_BLOCK, pw, 2 * KEY_BLOCK), BF16),
                   jax.ShapeDtypeStruct((b, n_pairs, nk // KEY_BLOCK, 2 * KEY_BLOCK, pw), BF16)],
        compiler_params=_params(2),
        name="kv_layout",
    )(k, v)


ATTN_EXIT = 104.0


def _attn_body(x_ref, g_ref, wq_ref, wo_ref, kd_ref, vd_ref, kp_ref, vp_ref, o_ref,
               q_ref, acc_ref, carry_ref, *, diag_block, blocks_before):
    qb = x_ref.shape[1]
    n_pairs = kd_ref.shape[1]
    pw = 2 * SB_HEAD_DIM
    tile = 2 * KEY_BLOCK
    q_per_kv = wq_ref.shape[1] // (SB_KV_HEADS * SB_HEAD_DIM)
    rows = q_per_kv * qb
    qi = pl.program_id(1)
    d_blk = qi if diag_block is None else diag_block
    last_past = (qi if blocks_before is None else jnp.int32(blocks_before)) - 1

    x = x_ref[0]
    h = _rms(x, g_ref[...]).astype(BF16)
    q = (_dot(h, wq_ref[...]) * (SB_HEAD_DIM ** -0.5)).astype(BF16)
    for pr in range(n_pairs):
        for g in range(q_per_kv):
            c0 = (pr * q_per_kv + g) * pw
            q_ref[pr, g * qb:(g + 1) * qb, :] = q[:, c0:c0 + pw]

    key_i = lax.broadcasted_iota(jnp.int32, (tile, tile), 0)
    key_s = lax.broadcasted_iota(jnp.int32, (tile, tile), 1)
    same_head = (key_i >= KEY_BLOCK) == (key_s >= KEY_BLOCK)
    suffix_ones = ((key_i >= key_s) & same_head).astype(BF16)
    row_q = lax.broadcasted_iota(jnp.int32, (rows, tile), 0) % qb
    col_k = lax.broadcasted_iota(jnp.int32, (rows, tile), 1) % KEY_BLOCK
    causal = col_k < row_q

    def sweep(pr, kt, vb, first):
        z = _dot(q_ref[pr], kt)
        sp = _softplus(z)
        if first:
            sp = jnp.where(causal, sp, 0.0)
        cs = _dot(sp.astype(BF16), suffix_ones)
        log_a = z - cs
        if first:
            log_a = jnp.where(causal, log_a, -jnp.inf)
        else:
            log_a = log_a - carry_ref[pr]
        pv = _dot(jnp.exp(log_a).astype(BF16), vb)
        total = jnp.concatenate(
            [jnp.broadcast_to(cs[:, 0:1], (rows, KEY_BLOCK)),
             jnp.broadcast_to(cs[:, KEY_BLOCK:KEY_BLOCK + 1], (rows, KEY_BLOCK))], axis=1)
        if first:
            acc_ref[pr] = pv
            carry_ref[pr] = total
        else:
            acc_ref[pr] += pv
            carry_ref[pr] += total

    for pr in range(n_pairs):
        sweep(pr, kd_ref[0, pr, d_blk], vd_ref[0, pr, d_blk], True)

    def more(state):
        kb, swept = state
        return jnp.logical_and(kb >= 0, swept < ATTN_EXIT)

    def step(state):
        kb, _ = state
        for pr in range(n_pairs):
            sweep(pr, kp_ref[0, pr, kb], vp_ref[0, pr, kb], False)
        return kb - 1, jnp.min(carry_ref[...])

    lax.while_loop(more, step, (last_past, jnp.min(carry_ref[...])))

    o = jnp.concatenate([acc_ref[pr, g * qb:(g + 1) * qb, :]
                         for pr in range(n_pairs) for g in range(q_per_kv)], axis=1)
    o_ref[0] = x + _dot(o.astype(BF16), wo_ref[...])


def _attn(x, g, wq, wo, kd, vd, kp, vp, diag_block, blocks_before):
    b, l, d = x.shape
    dq = wq.shape[1]
    qb = min(Q_BLOCK, l)
    n_pairs = SB_KV_HEADS // 2
    pw = 2 * SB_HEAD_DIM
    rows = (dq // (SB_KV_HEADS * SB_HEAD_DIM)) * qb

    def whole(a):
        return pl.BlockSpec((1,) + a.shape[1:], lambda i, j: (i, 0, 0, 0, 0))

    return pl.pallas_call(
        functools.partial(_attn_body, diag_block=diag_block, blocks_before=blocks_before),
        grid=(b, l // qb),
        in_specs=[pl.BlockSpec((1, qb, d), lambda i, j: (i, j, 0)),
                  _resident((1, d)), _resident((d, dq)), _resident((dq, d)),
                  whole(kd), whole(vd), whole(kp), whole(vp)],
        out_specs=pl.BlockSpec((1, qb, d), lambda i, j: (i, j, 0)),
        out_shape=jax.ShapeDtypeStruct((b, l, d), F32),
        scratch_shapes=[pltpu.VMEM((n_pairs, rows, pw), BF16),
                        pltpu.VMEM((n_pairs, rows, pw), F32),
                        pltpu.VMEM((n_pairs, rows, 2 * KEY_BLOCK), F32)],
        compiler_params=_params(2),
        name="sb_attn",
    )(x, g, wq, wo, kd, vd, kp, vp)


def _pair_major_order(n_heads):
    q_per_kv = n_heads // SB_KV_HEADS
    order = []
    for pr in range(SB_KV_HEADS // 2):
        for g in range(q_per_kv):
            for s in range(2):
                head = (2 * pr + s) * q_per_kv + g
                order.extend(range(head * SB_HEAD_DIM, (head + 1) * SB_HEAD_DIM))
    return jnp.asarray(order, jnp.int32)


def _mamba_body(x_ref, g_ref, wz_ref, wx_ref, wdt_ref, wdtt_ref, cw_ref, cb_ref,
                dtb_ref, dtbt_ref, alog_ref, alogt_ref, dskip_ref, ng_ref, wout_ref,
                s0_ref, c0_ref,
                o_ref, sfin_ref, cfin_ref,
                state_ref, pad_ref, y_ref, *, chunk, has_init):
    tt = x_ref.shape[1]
    d_inner = wz_ref.shape[1]
    n_heads = wdt_ref.shape[1]
    hpg = n_heads // SSM_GROUPS
    gw = hpg * SSM_HEAD_DIM
    gn = SSM_GROUPS * D_STATE
    t = pl.program_id(1)

    @pl.when(t == 0)
    def _init():
        pad_ref[0:CONV_PAD, :] = jnp.zeros((CONV_PAD, pad_ref.shape[1]), F32)
        if has_init:
            for g in range(SSM_GROUPS):
                state_ref[g] = s0_ref[0, g].T
            pad_ref[CONV_PAD - (CONV_W - 1):CONV_PAD, :] = c0_ref[0]
        else:
            state_ref[...] = jnp.zeros(state_ref.shape, F32)

    x = x_ref[0]
    h = _rms(x, g_ref[...]).astype(BF16)
    z = _dot(h, wz_ref[...])
    u = _dot(h, wx_ref[...])
    pad_ref[CONV_PAD:CONV_PAD + tt, :] = u
    conv = cb_ref[...]
    for k in range(CONV_W):
        off = CONV_PAD - (CONV_W - 1) + k
        conv = conv + pad_ref[off:off + tt, :] * cw_ref[k:k + 1, :]
    xbc = _silu(conv)
    xs = xbc[:, :d_inner]
    bm = xbc[:, d_inner:d_inner + gn].astype(BF16)
    cm = xbc[:, d_inner + gn:].astype(BF16)

    dt = _softplus(_dot(h, wdt_ref[...]) + dtb_ref[...])
    dtt = _softplus(_dot_nt(wdtt_ref[...], h) + dtbt_ref[...])
    a = dt * (-jnp.exp(alog_ref[...]))
    at = dtt * (-jnp.exp(alogt_ref[...]))

    ci = lax.broadcasted_iota(jnp.int32, (chunk, chunk), 0)
    cj = lax.broadcasted_iota(jnp.int32, (chunk, chunk), 1)
    lower = ci >= cj
    lower_f = lower.astype(F32)
    upper_f = (ci <= cj).astype(F32)

    for c in range(tt // chunk):
        r0 = c * chunk
        a_c = a[r0:r0 + chunk, :]
        acs = _dot_exact(lower_f, a_c)
        acst = _dot_exact(at[:, r0:r0 + chunk], upper_f)
        e_acs = jnp.exp(acs)
        acs_last = acs[chunk - 1:chunk, :]
        dw = dt[r0:r0 + chunk, :] * jnp.exp(acs_last - acs)
        e_last = jnp.exp(acs_last)
        dtt_c = dtt[:, r0:r0 + chunk]
        for g in range(SSM_GROUPS):
            b_c = bm[r0:r0 + chunk, g * D_STATE:(g + 1) * D_STATE]
            c_c = cm[r0:r0 + chunk, g * D_STATE:(g + 1) * D_STATE]
            cb = _dot_nt(c_c, b_c)
            s_g = state_ref[g]
            y_off = _dot(c_c, s_g.astype(BF16))
            xw = []
            dec = []
            for e in range(hpg):
                hd = g * hpg + e
                lo = hd * SSM_HEAD_DIM
                x_h = xs[r0:r0 + chunk, lo:lo + SSM_HEAD_DIM]
                seg = acs[:, hd:hd + 1] - acst[hd:hd + 1, :]
                decay = jnp.exp(jnp.where(lower, seg, -jnp.inf))
                m = (cb * decay * dtt_c[hd:hd + 1, :]).astype(BF16)
                y_h = _dot(m, x_h.astype(BF16))
                y_h = y_h + y_off[:, e * SSM_HEAD_DIM:(e + 1) * SSM_HEAD_DIM] * e_acs[:, hd:hd + 1]
                y_ref[r0:r0 + chunk, lo:lo + SSM_HEAD_DIM] = y_h
                xw.append(x_h * dw[:, hd:hd + 1])
                dec.append(jnp.broadcast_to(e_last[:, hd:hd + 1], (1, SSM_HEAD_DIM)))
            xw_g = jnp.concatenate(xw, axis=1).astype(BF16)
            dec_g = jnp.concatenate(dec, axis=1)
            state_ref[g] = s_g * dec_g + _dot_tn(b_c, xw_g)

    y = y_ref[...] + xs * dskip_ref[...]
    y = y * _silu(z)
    normed = []
    for g in range(SSM_GROUPS):
        yg = y[:, g * gw:(g + 1) * gw]
        normed.append(yg * lax.rsqrt(jnp.mean(yg * yg, axis=-1, keepdims=True) + NORM_EPS))
    y = jnp.concatenate(normed, axis=1) * ng_ref[...]
    o_ref[0] = x + _dot(y.astype(BF16), wout_ref[...])

    @pl.when(t == pl.num_programs(1) - 1)
    def _final():
        for g in range(SSM_GROUPS):
            sfin_ref[0, g] = state_ref[g].T
        cfin_ref[0] = pad_ref[CONV_PAD + tt - (CONV_W - 1):CONV_PAD + tt, :]

    pad_ref[0:CONV_PAD, :] = pad_ref[tt:tt + CONV_PAD, :]


def _mamba(x, g, w, s0, c0):
    b, l, d = x.shape
    d_inner = w["wz"].shape[1]
    conv_dim = w["wx"].shape[1]
    n_heads = w["wdt"].shape[1]
    gw = (n_heads // SSM_GROUPS) * SSM_HEAD_DIM
    tt = min(MAMBA_TILE, l)
    chunk = min(SSD_CHUNK, l)
    has_init = s0 is not None
    if not has_init:
        s0 = jnp.zeros((1, SSM_GROUPS, gw, D_STATE), F32)
        c0 = jnp.zeros((1, CONV_W - 1, conv_dim), F32)
        init_map = lambda i, j: (0, 0, 0, 0)
        conv_map = lambda i, j: (0, 0, 0)
    else:
        init_map = lambda i, j: (i, 0, 0, 0)
        conv_map = lambda i, j: (i, 0, 0)
    state_spec = pl.BlockSpec((1, SSM_GROUPS, gw, D_STATE), lambda i, j: (i, 0, 0, 0))
    conv_spec = pl.BlockSpec((1, CONV_W - 1, conv_dim), lambda i, j: (i, 0, 0))
    return pl.pallas_call(
        functools.partial(_mamba_body, chunk=chunk, has_init=has_init),
        grid=(b, l // tt),
        in_specs=[pl.BlockSpec((1, tt, d), lambda i, j: (i, j, 0)),
                  _resident((1, d)), _resident((d, d_inner)), _resident((d, conv_dim)),
                  _resident((d, n_heads)), _resident((n_heads, d)),
                  _resident((CONV_W, conv_dim)), _resident((1, conv_dim)),
                  _resident((1, n_heads)), _resident((n_heads, 1)),
                  _resident((1, n_heads)), _resident((n_heads, 1)),
                  _resident((1, d_inner)), _resident((1, d_inner)), _resident((d_inner, d)),
                  pl.BlockSpec((1, SSM_GROUPS, gw, D_STATE), init_map),
                  pl.BlockSpec((1, CONV_W - 1, conv_dim), conv_map)],
        out_specs=[pl.BlockSpec((1, tt, d), lambda i, j: (i, j, 0)), state_spec, conv_spec],
        out_shape=[jax.ShapeDtypeStruct((b, l, d), F32),
                   jax.ShapeDtypeStruct((b, SSM_GROUPS, gw, D_STATE), F32),
                   jax.ShapeDtypeStruct((b, CONV_W - 1, conv_dim), F32)],
        scratch_shapes=[pltpu.VMEM((SSM_GROUPS, D_STATE, gw), F32),
                        pltpu.VMEM((CONV_PAD + tt, conv_dim), F32),
                        pltpu.VMEM((tt, d_inner), F32)],
        compiler_params=_params(2),
        name="mamba2",
    )(x, g, w["wz"], w["wx"], w["wdt"], w["wdtt"], w["conv_w"], w["conv_b"],
      w["dt_bias"], w["dt_bias_t"], w["a_log"], w["a_log_t"], w["d_skip"], w["norm_g"],
      w["wout"], s0, c0)


def _prepare_mamba(w_in, conv_w, conv_b, dt_bias, a_log, d_skip, norm_g, w_out):
    d_inner = w_out.shape[0]
    n_heads = dt_bias.shape[0]
    conv_dim = conv_w.shape[1]
    wdt = w_in[:, d_inner + conv_dim:].astype(BF16)
    return {
        "wz": w_in[:, :d_inner].astype(BF16),
        "wx": w_in[:, d_inner:d_inner + conv_dim].astype(BF16),
        "wdt": wdt,
        "wdtt": wdt.T,
        "conv_w": conv_w,
        "conv_b": conv_b.reshape(1, conv_dim),
        "dt_bias": dt_bias.reshape(1, n_heads),
        "dt_bias_t": dt_bias.reshape(n_heads, 1),
        "a_log": a_log.reshape(1, n_heads),
        "a_log_t": a_log.reshape(n_heads, 1),
        "d_skip": jnp.repeat(d_skip, SSM_HEAD_DIM).reshape(1, d_inner),
        "norm_g": norm_g.reshape(1, d_inner),
        "wout": w_out.astype(BF16),
    }


def _trunk(x, p, ssm0, conv0, k_past, v_past, wts):
    b, l, d = x.shape
    t = b * l
    depth = wts["ffn_norm"].shape[0]
    n_a = len(wts["mamba"])
    ssm_out, conv_out = [], []
    k_new = v_new = kd = vd = kp = vp = None
    diag_block = blocks_before = None
    xf = x.reshape(t, d)
    for i in range(depth):
        if i == n_a:
            k_new, v_new = _kv(xf, wts["kv_norm"], wts["w_k"], wts["w_v"])
            k3 = k_new.reshape(b, l, -1)
            v3 = v_new.reshape(b, l, -1)
            if k_past is None:
                kd, vd = _kv_layout(k3, v3)
                kp, vp = kd, vd
            else:
                past = k_past.shape[1]
                assert past % KEY_BLOCK == 0 and l <= KEY_BLOCK
                fill = ((0, 0), (0, KEY_BLOCK - l), (0, 0))
                kd, vd = _kv_layout(jnp.pad(k3, fill), jnp.pad(v3, fill))
                kp, vp = _kv_layout(k_past.reshape(b, past, -1), v_past.reshape(b, past, -1))
                diag_block, blocks_before = 0, past // KEY_BLOCK
        f = wts["ffn"][i][0]
        xf = _ffn(xf, f["g"], f["wg"], f["wu"], f["wd"])
        if i < n_a:
            s0 = None if ssm0 is None else ssm0[i].reshape(b, SSM_GROUPS, -1, D_STATE)
            c0 = None if conv0 is None else conv0[i]
            x3, s_fin, c_fin = _mamba(xf.reshape(b, l, d), wts["mix_norm"][i], wts["mamba"][i], s0, c0)
            xf = x3.reshape(t, d)
            ssm_out.append(s_fin.reshape(b, -1, SSM_HEAD_DIM, D_STATE))
            conv_out.append(c_fin)
        else:
            jb = i - n_a
            x3 = _attn(xf.reshape(b, l, d), wts["mix_norm"][i], wts["wq"][jb], wts["wo"][jb],
                       kd, vd, kp, vp, diag_block, blocks_before)
            xf = x3.reshape(t, d)
        f = wts["ffn"][i][1]
        xf = _ffn(xf, f["g"], f["wg"], f["wu"], f["wd"])
        e = wts["ple"][i]
        xf = _ple(xf, p[i].reshape(t, -1), e["g"], e["wgate"], e["wproj"], wts["final_norm"],
                  final=(i == depth - 1))
    kv_shape = (b, l, SB_KV_HEADS, SB_HEAD_DIM)
    return (xf.reshape(b, l, d), jnp.stack(ssm_out), jnp.stack(conv_out),
            k_new.reshape(kv_shape), v_new.reshape(kv_shape))


def kernel(x_prompt, x_sample, p_prompt, p_sample, state_ssm, state_conv, cache_k, cache_v,
           ffn_norm, ffn_w_gate, ffn_w_up, ffn_w_down, mix_norm,
           ssm_w_in, ssm_conv_w, ssm_conv_b, ssm_dt_bias, ssm_a_log, ssm_d, ssm_norm, ssm_w_out,
           kv_norm, w_k, w_v, sb_w_q, sb_w_o,
           ple_norm, ple_w_gate, ple_w_proj, final_norm):
    depth, d = mix_norm.shape
    n_a = ssm_w_in.shape[0]
    n_b = sb_w_q.shape[0]
    head_order = _pair_major_order(sb_w_q.shape[2] // SB_HEAD_DIM)
    wts = {
        "ffn_norm": ffn_norm,
        "ffn": [[{"g": ffn_norm[i, s].reshape(1, d),
                  "wg": ffn_w_gate[i, s].astype(BF16),
                  "wu": ffn_w_up[i, s].astype(BF16),
                  "wd": ffn_w_down[i, s].astype(BF16)} for s in range(2)] for i in range(depth)],
        "mix_norm": [mix_norm[i].reshape(1, d) for i in range(depth)],
        "mamba": [_prepare_mamba(ssm_w_in[i], ssm_conv_w[i], ssm_conv_b[i], ssm_dt_bias[i],
                                 ssm_a_log[i], ssm_d[i], ssm_norm[i], ssm_w_out[i])
                  for i in range(n_a)],
        "kv_norm": kv_norm.reshape(1, d),
        "w_k": w_k.astype(BF16),
        "w_v": w_v.astype(BF16),
        "wq": [sb_w_q[j][:, head_order].astype(BF16) for j in range(n_b)],
        "wo": [sb_w_o[j][head_order, :].astype(BF16) for j in range(n_b)],
        "ple": [{"g": ple_norm[i].reshape(1, d),
                 "wgate": ple_w_gate[i].astype(BF16),
                 "wproj": ple_w_proj[i].astype(BF16)} for i in range(depth)],
        "final_norm": final_norm.reshape(1, d),
    }
    y_p, ssm_p, conv_p, k_p, v_p = _trunk(x_prompt, p_prompt, None, None, None, None, wts)
    y_s, ssm_s, conv_s, k_s, v_s = _trunk(x_sample, p_sample, state_ssm, state_conv,
                                          cache_k, cache_v, wts)
    return (y_p, y_s, ssm_p, conv_p, k_p, v_p, ssm_s, conv_s, k_s, v_s)
```

```python
import functools

import jax
import jax.numpy as jnp
from jax import lax
from jax.experimental import pallas as pl
from jax.experimental.pallas import tpu as pltpu

F32 = jnp.float32
BF16 = jnp.bfloat16
NORM_EPS = 1e-6

SSM_HEAD_DIM = 64
SSM_GROUPS = 4
D_STATE = 128
CONV_W = 4
SSD_CHUNK = 64
SB_HEAD_DIM = 64
SB_KV_HEADS = 4
Q_BLOCK = 128
KEY_BLOCK = 128

VMEM_LIMIT_BYTES = 56 * 1024 * 1024
TOKEN_TILE = 512
MAMBA_TILE = 256
MXU_TILE = 256
CONV_PAD = 8


def _rms(x, g):
    return x * lax.rsqrt(jnp.mean(x * x, axis=-1, keepdims=True) + NORM_EPS) * g


def _sigmoid(v):
    return 1.0 / (1.0 + jnp.exp(-v))


def _silu(v):
    return v * _sigmoid(v)


def _softplus(v):
    return jnp.maximum(v, 0.0) + jnp.log(1.0 + jnp.exp(-jnp.abs(v)))


def _dot(a, b):
    return jnp.dot(a, b, preferred_element_type=F32)


def _dot_nt(a, b):
    return lax.dot_general(a, b, (((1,), (1,)), ((), ())), preferred_element_type=F32)


def _dot_tn(a, b):
    return lax.dot_general(a, b, (((0,), (0,)), ((), ())), preferred_element_type=F32)


def _dot_exact(a, b):
    return jnp.dot(a, b, precision=lax.Precision.HIGHEST, preferred_element_type=F32)


def _resident(shape):
    zeros = (0,) * len(shape)
    return pl.BlockSpec(shape, lambda *_: zeros, pipeline_mode=pl.Buffered(1))


def _params(n_axes):
    return pltpu.CompilerParams(dimension_semantics=("arbitrary",) * n_axes,
                                vmem_limit_bytes=VMEM_LIMIT_BYTES)


def _ffn_chunks(d_ff):
    if d_ff % MXU_TILE:
        return [(0, d_ff)]
    tiles = d_ff // MXU_TILE
    first = (tiles + 1) // 2 * MXU_TILE
    return [(0, first), (first, d_ff)] if first < d_ff else [(0, d_ff)]


def _token_body(*refs, with_ple, final, with_kv):
    it = iter(refs)
    x_ref, g_ref, wg_ref, wu_ref, wd_ref = (next(it) for _ in range(5))
    if with_ple:
        p_ref, pg_ref, wgate_ref, wproj_ref = (next(it) for _ in range(4))
    if final:
        fin_ref = next(it)
    if with_kv:
        kvg_ref, wk_ref, wv_ref = (next(it) for _ in range(3))
    o_ref = next(it)

    x = x_ref[...]
    h = _rms(x, g_ref[...]).astype(BF16)
    acc = jnp.zeros_like(x)
    for lo, hi in _ffn_chunks(wg_ref.shape[1]):
        gate = _dot(h, wg_ref[:, lo:hi])
        up = _dot(h, wu_ref[:, lo:hi])
        act = (_silu(gate) * up).astype(BF16)
        acc = acc + _dot(act, wd_ref[lo:hi, :])
    y = x + 0.5 * acc
    if with_ple:
        hp = _rms(y, pg_ref[...]).astype(BF16)
        y = y + _sigmoid(_dot(hp, wgate_ref[...])) * _dot(p_ref[...].astype(BF16), wproj_ref[...])
    if final:
        y = _rms(y, fin_ref[...])
    o_ref[...] = y
    if with_kv:
        k_ref, v_ref = next(it), next(it)
        hk = _rms(y, kvg_ref[...]).astype(BF16)
        k_ref[...] = _dot(hk, wk_ref[...])
        v_ref[...] = _dot(hk, wv_ref[...])


def _token_stage(x, ffn, ple=None, p=None, final_g=None, kv=None):
    t, d = x.shape
    tm = min(TOKEN_TILE, t)
    row = lambda n: pl.BlockSpec((tm, n), lambda i: (i, 0))
    args = [x, ffn["g"], ffn["wg"], ffn["wu"], ffn["wd"]]
    in_specs = [row(d)] + [_resident(a.shape) for a in args[1:]]
    if ple is not None:
        extra = [ple["g"], ple["wgate"], ple["wproj"]]
        args += [p] + extra
        in_specs += [row(p.shape[1])] + [_resident(a.shape) for a in extra]
    if final_g is not None:
        args.append(final_g)
        in_specs.append(_resident(final_g.shape))
    out_specs = [row(d)]
    out_shape = [jax.ShapeDtypeStruct((t, d), F32)]
    if kv is not None:
        extra = [kv["g"], kv["wk"], kv["wv"]]
        args += extra
        in_specs += [_resident(a.shape) for a in extra]
        dkv = kv["wk"].shape[1]
        out_specs += [row(dkv), row(dkv)]
        out_shape += [jax.ShapeDtypeStruct((t, dkv), F32)] * 2
    return pl.pallas_call(
        functools.partial(_token_body, with_ple=ple is not None, final=final_g is not None,
                          with_kv=kv is not None),
        grid=(t // tm,),
        in_specs=in_specs,
        out_specs=out_specs,
        out_shape=out_shape,
        compiler_params=_params(1),
        name="token_stage",
    )(*args)


def _kv_layout_body(k_ref, v_ref, kt_ref, vb_ref):
    tm = k_ref.shape[1]
    hd = SB_HEAD_DIM
    pw = 2 * hd
    kt = k_ref[0].T.astype(BF16)
    v = v_ref[0].astype(BF16)
    first_head = lax.broadcasted_iota(jnp.int32, (KEY_BLOCK, pw), 1) < hd
    zeros_k = jnp.zeros((hd, KEY_BLOCK), BF16)
    zeros_v = jnp.zeros((KEY_BLOCK, pw), BF16)
    for pr in range(SB_KV_HEADS // 2):
        for kb in range(tm // KEY_BLOCK):
            keys = slice(kb * KEY_BLOCK, (kb + 1) * KEY_BLOCK)
            kt_ref[0, pr, kb, 0:hd, 0:KEY_BLOCK] = kt[(2 * pr) * hd:(2 * pr + 1) * hd, keys]
            kt_ref[0, pr, kb, 0:hd, KEY_BLOCK:2 * KEY_BLOCK] = zeros_k
            kt_ref[0, pr, kb, hd:pw, 0:KEY_BLOCK] = zeros_k
            kt_ref[0, pr, kb, hd:pw, KEY_BLOCK:2 * KEY_BLOCK] = kt[(2 * pr + 1) * hd:(2 * pr + 2) * hd, keys]
            vp = v[keys, pr * pw:(pr + 1) * pw]
            vb_ref[0, pr, kb, 0:KEY_BLOCK, :] = jnp.where(first_head, vp, zeros_v)
            vb_ref[0, pr, kb, KEY_BLOCK:2 * KEY_BLOCK, :] = jnp.where(first_head, zeros_v, vp)


def _kv_layout(k, v):
    b, nk, dkv = k.shape
    tm = min(TOKEN_TILE, nk)
    n_pairs = SB_KV_HEADS // 2
    pw = 2 * SB_HEAD_DIM
    nb = tm // KEY_BLOCK
    return pl.pallas_call(
        _kv_layout_body,
        grid=(b, nk // tm),
        in_specs=[pl.BlockSpec((1, tm, dkv), lambda i, j: (i, j, 0)),
                  pl.BlockSpec((1, tm, dkv), lambda i, j: (i, j, 0))],
        out_specs=[pl.BlockSpec((1, n_pairs, nb, pw, 2 * KEY_BLOCK), lambda i, j: (i, 0, j, 0, 0)),
                   pl.BlockSpec((1, n_pairs, nb, 2 * KEY_BLOCK, pw), lambda i, j: (i, 0, j, 0, 0))],
        out_shape=[jax.ShapeDtypeStruct((b, n_pairs, nk // KEY_BLOCK, pw, 2 * KEY_BLOCK), BF16),
                   jax.ShapeDtypeStruct((b, n_pairs, nk // KEY_BLOCK, 2 * KEY_BLOCK, pw), BF16)],
        compiler_params=_params(2),
        name="kv_layout",
    )(k, v)


ATTN_EXIT = 104.0


def _attn_body(x_ref, g_ref, wq_ref, wo_ref, kd_ref, vd_ref, kp_ref, vp_ref, *refs,
               diag_block, blocks_before, report_swept):
    if report_swept:
        o_ref, swept_ref, q_ref, acc_ref, carry_ref = refs
    else:
        o_ref, q_ref, acc_ref, carry_ref = refs
    qb = x_ref.shape[1]
    n_pairs = kd_ref.shape[1]
    pw = 2 * SB_HEAD_DIM
    tile = 2 * KEY_BLOCK
    q_per_kv = wq_ref.shape[1] // (SB_KV_HEADS * SB_HEAD_DIM)
    rows = q_per_kv * qb
    qi = pl.program_id(1)
    d_blk = qi if diag_block is None else diag_block
    last_past = (qi if blocks_before is None else jnp.int32(blocks_before)) - 1

    x = x_ref[0]
    h = _rms(x, g_ref[...]).astype(BF16)
    q = (_dot(h, wq_ref[...]) * (SB_HEAD_DIM ** -0.5)).astype(BF16)
    for pr in range(n_pairs):
        for g in range(q_per_kv):
            c0 = (pr * q_per_kv + g) * pw
            q_ref[pr, g * qb:(g + 1) * qb, :] = q[:, c0:c0 + pw]

    key_i = lax.broadcasted_iota(jnp.int32, (tile, tile), 0)
    key_s = lax.broadcasted_iota(jnp.int32, (tile, tile), 1)
    same_head = (key_i >= KEY_BLOCK) == (key_s >= KEY_BLOCK)
    suffix_ones = ((key_i >= key_s) & same_head).astype(BF16)
    row_q = lax.broadcasted_iota(jnp.int32, (rows, tile), 0) % qb
    col_k = lax.broadcasted_iota(jnp.int32, (rows, tile), 1) % KEY_BLOCK
    causal = col_k < row_q

    def sweep(pr, kt, vb, first):
        z = _dot(q_ref[pr], kt)
        sp = _softplus(z)
        if first:
            sp = jnp.where(causal, sp, 0.0)
        cs = _dot(sp.astype(BF16), suffix_ones)
        log_a = z - cs
        if first:
            log_a = jnp.where(causal, log_a, -jnp.inf)
        else:
            log_a = log_a - carry_ref[pr]
        pv = _dot(jnp.exp(log_a).astype(BF16), vb)
        total = jnp.concatenate(
            [jnp.broadcast_to(cs[:, 0:1], (rows, KEY_BLOCK)),
             jnp.broadcast_to(cs[:, KEY_BLOCK:KEY_BLOCK + 1], (rows, KEY_BLOCK))], axis=1)
        if first:
            acc_ref[pr] = pv
            carry_ref[pr] = total
        else:
            acc_ref[pr] += pv
            carry_ref[pr] += total

    for pr in range(n_pairs):
        sweep(pr, kd_ref[0, pr, d_blk], vd_ref[0, pr, d_blk], True)

    def more(state):
        kb, swept = state
        return jnp.logical_and(kb >= 0, swept < ATTN_EXIT)

    def step(state):
        kb, _ = state
        for pr in range(n_pairs):
            sweep(pr, kp_ref[0, pr, kb], vp_ref[0, pr, kb], False)
        return kb - 1, jnp.min(carry_ref[...])

    _, swept = lax.while_loop(more, step, (last_past, jnp.min(carry_ref[...])))
    if report_swept:
        swept_ref[0, 0] = jnp.full(swept_ref.shape[2:], swept, F32)

    o = jnp.concatenate([acc_ref[pr, g * qb:(g + 1) * qb, :]
                         for pr in range(n_pairs) for g in range(q_per_kv)], axis=1)
    o_ref[0] = x + _dot(o.astype(BF16), wo_ref[...])


def _attn(x, g, wq, wo, kd, vd, kp, vp, diag_block, blocks_before, report_swept=False):
    b, l, d = x.shape
    dq = wq.shape[1]
    qb = min(Q_BLOCK, l)
    n_pairs = SB_KV_HEADS // 2
    pw = 2 * SB_HEAD_DIM
    rows = (dq // (SB_KV_HEADS * SB_HEAD_DIM)) * qb

    def whole(a):
        return pl.BlockSpec((1,) + a.shape[1:], lambda i, j: (i, 0, 0, 0, 0))

    out_specs = [pl.BlockSpec((1, qb, d), lambda i, j: (i, j, 0))]
    out_shape = [jax.ShapeDtypeStruct((b, l, d), F32)]
    if report_swept:
        out_specs.append(pl.BlockSpec((1, 1, 8, 128), lambda i, j: (i, j, 0, 0)))
        out_shape.append(jax.ShapeDtypeStruct((b, l // qb, 8, 128), F32))
    return pl.pallas_call(
        functools.partial(_attn_body, diag_block=diag_block, blocks_before=blocks_before,
                          report_swept=report_swept),
        grid=(b, l // qb),
        in_specs=[pl.BlockSpec((1, qb, d), lambda i, j: (i, j, 0)),
                  _resident((1, d)), _resident((d, dq)), _resident((dq, d)),
                  whole(kd), whole(vd), whole(kp), whole(vp)],
        out_specs=out_specs,
        out_shape=out_shape,
        scratch_shapes=[pltpu.VMEM((n_pairs, rows, pw), BF16),
                        pltpu.VMEM((n_pairs, rows, pw), F32),
                        pltpu.VMEM((n_pairs, rows, 2 * KEY_BLOCK), F32)],
        compiler_params=_params(2),
        name="sb_attn",
    )(x, g, wq, wo, kd, vd, kp, vp)


def _pair_major_order(n_heads):
    q_per_kv = n_heads // SB_KV_HEADS
    order = []
    for pr in range(SB_KV_HEADS // 2):
        for g in range(q_per_kv):
            for s in range(2):
                head = (2 * pr + s) * q_per_kv + g
                order.extend(range(head * SB_HEAD_DIM, (head + 1) * SB_HEAD_DIM))
    return jnp.asarray(order, jnp.int32)


def _mamba_body(x_ref, g_ref, wz_ref, wx_ref, wdt_ref, wdtt_ref, cw_ref, cb_ref,
                dtb_ref, dtbt_ref, alog_ref, alogt_ref, dskip_ref, ng_ref, wout_ref,
                s0_ref, c0_ref,
                o_ref, sfin_ref, cfin_ref,
                state_ref, pad_ref, y_ref, xbd_ref, *, chunk, valid, has_init):
    tt = x_ref.shape[1]
    d_inner = wz_ref.shape[1]
    n_heads = wdt_ref.shape[1]
    hpg = n_heads // SSM_GROUPS
    gw = hpg * SSM_HEAD_DIM
    gn = SSM_GROUPS * D_STATE
    t = pl.program_id(1)

    @pl.when(t == 0)
    def _init():
        pad_ref[0:CONV_PAD, :] = jnp.zeros((CONV_PAD, pad_ref.shape[1]), F32)
        xbd_ref[...] = jnp.zeros(xbd_ref.shape, BF16)
        if has_init:
            for g in range(SSM_GROUPS):
                state_ref[g] = s0_ref[0, g].T
            pad_ref[CONV_PAD - (CONV_W - 1):CONV_PAD, :] = c0_ref[0]
        else:
            state_ref[...] = jnp.zeros(state_ref.shape, F32)

    x = x_ref[0]
    h = _rms(x, g_ref[...]).astype(BF16)
    z = _dot(h, wz_ref[...])
    u = _dot(h, wx_ref[...])
    pad_ref[CONV_PAD:CONV_PAD + tt, :] = u
    conv = cb_ref[...] + u * cw_ref[CONV_W - 1:CONV_W, :]
    for k in range(CONV_W - 2, -1, -1):
        off = CONV_PAD - (CONV_W - 1) + k
        conv = conv + pad_ref[off:off + tt, :] * cw_ref[k:k + 1, :]
    xbc = _silu(conv)
    xs = xbc[:, :d_inner]
    bm = xbc[:, d_inner:d_inner + gn].astype(BF16)
    cm = xbc[:, d_inner + gn:].astype(BF16)

    dt = _softplus(_dot(h, wdt_ref[...]) + dtb_ref[...])
    dtt = _softplus(_dot_nt(wdtt_ref[...], h) + dtbt_ref[...])
    if valid < tt:
        dt = jnp.where(lax.broadcasted_iota(jnp.int32, dt.shape, 0) < valid, dt, 0.0)
        dtt = jnp.where(lax.broadcasted_iota(jnp.int32, dtt.shape, 1) < valid, dtt, 0.0)
    a = dt * (-jnp.exp(alog_ref[...]))
    at = dtt * (-jnp.exp(alogt_ref[...]))

    hd = SSM_HEAD_DIM
    pw, qw = 2 * hd, 4 * hd
    half = n_heads // 2
    ci = lax.broadcasted_iota(jnp.int32, (chunk, chunk), 0)
    cj = lax.broadcasted_iota(jnp.int32, (chunk, chunk), 1)
    lower_f = (ci >= cj).astype(F32)
    upper_f = (ci <= cj).astype(F32)
    lane_p = lax.broadcasted_iota(jnp.int32, (chunk, pw), 1)
    first = lane_p < hd
    lower2 = lax.broadcasted_iota(jnp.int32, (chunk, pw), 0) >= lane_p % hd

    def pair_cols(v, p):
        return jnp.where(first, v[:, 2 * p:2 * p + 1], v[:, 2 * p + 1:2 * p + 2])

    xs_bf = xs.astype(BF16)
    quads_per_group = hpg // 4
    for c in range(tt // chunk):
        rows = slice(c * chunk, (c + 1) * chunk)
        acs = _dot_exact(lower_f, a[rows, :])
        acst = _dot_exact(at[:, rows], upper_f)
        acst_p = jnp.concatenate([acst[:half], acst[half:]], axis=1)
        dtt_p = jnp.concatenate([dtt[:half, rows], dtt[half:, rows]], axis=1)
        dw = dt[rows, :] * jnp.exp(acs[chunk - 1:chunk, :] - acs)
        for g in range(SSM_GROUPS):
            b_c = bm[rows, g * D_STATE:(g + 1) * D_STATE]
            c_c = cm[rows, g * D_STATE:(g + 1) * D_STATE]
            cb2 = _dot_nt(c_c, jnp.concatenate([b_c, b_c], axis=0))
            s_g = state_ref[g]
            y_off = _dot(c_c, s_g.astype(BF16))
            xw, dec = [], []
            for q in range(quads_per_group):
                quad = g * quads_per_group + q
                lo = quad * qw
                m_q, e_q = [], []
                for p in (2 * quad, 2 * quad + 1):
                    col = pair_cols(acs, p)
                    seg = col - acst_p[p:p + 1, :]
                    decay = jnp.exp(jnp.where(lower2, seg, -jnp.inf))
                    m_q.append((cb2 * decay * dtt_p[p:p + 1, :]).astype(BF16))
                    e_q.append(jnp.exp(col))
                    dec.append(jnp.exp(col[chunk - 1:chunk, :]))
                    xw.append((xs[rows, p * pw:(p + 1) * pw] * pair_cols(dw, p)).astype(BF16))
                slot = c * (n_heads // 4) + quad
                for k in range(4):
                    xbd_ref[slot, k * chunk:(k + 1) * chunk, k * hd:(k + 1) * hd] = (
                        xs_bf[rows, lo + k * hd:lo + (k + 1) * hd])
                y_q = _dot(jnp.concatenate(m_q, axis=1), xbd_ref[slot])
                y_q = y_q + y_off[:, q * qw:(q + 1) * qw] * jnp.concatenate(e_q, axis=1)
                y_ref[rows, lo:lo + qw] = y_q
            state_ref[g] = (s_g * jnp.concatenate(dec, axis=1)
                            + _dot_tn(b_c, jnp.concatenate(xw, axis=1)))

    y = y_ref[...] + xs * dskip_ref[...]
    y = y * _silu(z)
    normed = []
    for g in range(SSM_GROUPS):
        yg = y[:, g * gw:(g + 1) * gw]
        normed.append(yg * lax.rsqrt(jnp.mean(yg * yg, axis=-1, keepdims=True) + NORM_EPS))
    y = jnp.concatenate(normed, axis=1) * ng_ref[...]
    o_ref[0] = x + _dot(y.astype(BF16), wout_ref[...])

    @pl.when(t == pl.num_programs(1) - 1)
    def _final():
        for g in range(SSM_GROUPS):
            sfin_ref[0, g] = state_ref[g].T
        cfin_ref[0] = pad_ref[CONV_PAD + valid - (CONV_W - 1):CONV_PAD + valid, :]

    pad_ref[0:CONV_PAD, :] = pad_ref[tt:tt + CONV_PAD, :]


def _mamba(x, g, w, s0, c0):
    b, l, d = x.shape
    d_inner = w["wz"].shape[1]
    conv_dim = w["wx"].shape[1]
    n_heads = w["wdt"].shape[1]
    gw = (n_heads // SSM_GROUPS) * SSM_HEAD_DIM
    chunk = SSD_CHUNK
    assert chunk == SSM_HEAD_DIM and (n_heads // SSM_GROUPS) % 4 == 0
    valid = l
    if l < chunk:
        x = jnp.pad(x, ((0, 0), (0, chunk - l), (0, 0)))
    l_pad = x.shape[1]
    tt = min(MAMBA_TILE, l_pad)
    assert l_pad % tt == 0 and tt % chunk == 0
    if l_pad == l:
        valid = tt
    has_init = s0 is not None
    if not has_init:
        s0 = jnp.zeros((1, SSM_GROUPS, gw, D_STATE), F32)
        c0 = jnp.zeros((1, CONV_W - 1, conv_dim), F32)
        init_map = lambda i, j: (0, 0, 0, 0)
        conv_map = lambda i, j: (0, 0, 0)
    else:
        init_map = lambda i, j: (i, 0, 0, 0)
        conv_map = lambda i, j: (i, 0, 0)
    state_spec = pl.BlockSpec((1, SSM_GROUPS, gw, D_STATE), lambda i, j: (i, 0, 0, 0))
    conv_spec = pl.BlockSpec((1, CONV_W - 1, conv_dim), lambda i, j: (i, 0, 0))
    n_slots = (tt // chunk) * (n_heads // 4)
    out, s_fin, c_fin = pl.pallas_call(
        functools.partial(_mamba_body, chunk=chunk, valid=valid, has_init=has_init),
        grid=(b, l_pad // tt),
        in_specs=[pl.BlockSpec((1, tt, d), lambda i, j: (i, j, 0)),
                  _resident((1, d)), _resident((d, d_inner)), _resident((d, conv_dim)),
                  _resident((d, n_heads)), _resident((n_heads, d)),
                  _resident((CONV_W, conv_dim)), _resident((1, conv_dim)),
                  _resident((1, n_heads)), _resident((n_heads, 1)),
                  _resident((1, n_heads)), _resident((n_heads, 1)),
                  _resident((1, d_inner)), _resident((1, d_inner)), _resident((d_inner, d)),
                  pl.BlockSpec((1, SSM_GROUPS, gw, D_STATE), init_map),
                  pl.BlockSpec((1, CONV_W - 1, conv_dim), conv_map)],
        out_specs=[pl.BlockSpec((1, tt, d), lambda i, j: (i, j, 0)), state_spec, conv_spec],
        out_shape=[jax.ShapeDtypeStruct((b, l_pad, d), F32),
                   jax.ShapeDtypeStruct((b, SSM_GROUPS, gw, D_STATE), F32),
                   jax.ShapeDtypeStruct((b, CONV_W - 1, conv_dim), F32)],
        scratch_shapes=[pltpu.VMEM((SSM_GROUPS, D_STATE, gw), F32),
                        pltpu.VMEM((CONV_PAD + tt, conv_dim), F32),
                        pltpu.VMEM((tt, d_inner), F32),
                        pltpu.VMEM((n_slots, 4 * chunk, 4 * SSM_HEAD_DIM), BF16)],
        compiler_params=_params(2),
        name="mamba2",
    )(x, g, w["wz"], w["wx"], w["wdt"], w["wdtt"], w["conv_w"], w["conv_b"],
      w["dt_bias"], w["dt_bias_t"], w["a_log"], w["a_log_t"], w["d_skip"], w["norm_g"],
      w["wout"], s0, c0)
    return out[:, :l], s_fin, c_fin


def _prepare_mamba(w_in, conv_w, conv_b, dt_bias, a_log, d_skip, norm_g, w_out):
    d_inner = w_out.shape[0]
    n_heads = dt_bias.shape[0]
    conv_dim = conv_w.shape[1]
    wdt = w_in[:, d_inner + conv_dim:].astype(BF16)
    eo = jnp.concatenate([jnp.arange(0, n_heads, 2), jnp.arange(1, n_heads, 2)])
    return {
        "wz": w_in[:, :d_inner].astype(BF16),
        "wx": w_in[:, d_inner:d_inner + conv_dim].astype(BF16),
        "wdt": wdt,
        "wdtt": wdt.T[eo],
        "conv_w": conv_w,
        "conv_b": conv_b.reshape(1, conv_dim),
        "dt_bias": dt_bias.reshape(1, n_heads),
        "dt_bias_t": dt_bias[eo].reshape(n_heads, 1),
        "a_log": a_log.reshape(1, n_heads),
        "a_log_t": a_log[eo].reshape(n_heads, 1),
        "d_skip": jnp.repeat(d_skip, SSM_HEAD_DIM).reshape(1, d_inner),
        "norm_g": norm_g.reshape(1, d_inner),
        "wout": w_out.astype(BF16),
    }


NEAR_PAST_BLOCKS = 4


def _attn_with_past(layer, k_past, v_past):
    b, past = k_past.shape[:2]

    def sweep(n_keys, report):
        kp, vp = _kv_layout(k_past[:, past - n_keys:].reshape(b, n_keys, -1),
                            v_past[:, past - n_keys:].reshape(b, n_keys, -1))
        return _attn(*layer, kp, vp, 0, n_keys // KEY_BLOCK, report_swept=report)

    near = min(NEAR_PAST_BLOCKS * KEY_BLOCK, past)
    if near == past:
        return sweep(past, False)[0]
    out_near, swept = sweep(near, True)
    return lax.cond(jnp.min(swept) >= ATTN_EXIT, lambda: out_near, lambda: sweep(past, False)[0])


def _trunk(x, p, ssm0, conv0, k_past, v_past, wts):
    b, l, d = x.shape
    t = b * l
    depth = wts["ffn_norm"].shape[0]
    n_a = len(wts["mamba"])
    assert 1 <= n_a < depth
    ssm_out, conv_out = [], []
    k_new = v_new = kd = vd = None
    xf = x.reshape(t, d)
    for i in range(depth):
        xf, = _token_stage(xf, wts["ffn"][i][0])
        if i < n_a:
            s0 = None if ssm0 is None else ssm0[i].reshape(b, SSM_GROUPS, -1, D_STATE)
            c0 = None if conv0 is None else conv0[i]
            x3, s_fin, c_fin = _mamba(xf.reshape(b, l, d), wts["mix_norm"][i], wts["mamba"][i], s0, c0)
            xf = x3.reshape(t, d)
            ssm_out.append(s_fin.reshape(b, -1, SSM_HEAD_DIM, D_STATE))
            conv_out.append(c_fin)
        else:
            jb = i - n_a
            layer = (xf.reshape(b, l, d), wts["mix_norm"][i], wts["wq"][jb], wts["wo"][jb], kd, vd)
            if k_past is None:
                x3, = _attn(*layer, kd, vd, None, None)
            else:
                x3 = _attn_with_past(layer, k_past, v_past)
            xf = x3.reshape(t, d)
        outs = _token_stage(xf, wts["ffn"][i][1], ple=wts["ple"][i], p=p[i].reshape(t, -1),
                            final_g=wts["final_norm"] if i == depth - 1 else None,
                            kv=wts["kv"] if i + 1 == n_a else None)
        xf = outs[0]
        if i + 1 == n_a:
            k_new, v_new = outs[1], outs[2]
            k3 = k_new.reshape(b, l, -1)
            v3 = v_new.reshape(b, l, -1)
            if k_past is None:
                kd, vd = _kv_layout(k3, v3)
            else:
                assert k_past.shape[1] % KEY_BLOCK == 0 and l <= KEY_BLOCK
                fill = ((0, 0), (0, KEY_BLOCK - l), (0, 0))
                kd, vd = _kv_layout(jnp.pad(k3, fill), jnp.pad(v3, fill))
    kv_shape = (b, l, SB_KV_HEADS, SB_HEAD_DIM)
    return (xf.reshape(b, l, d), jnp.stack(ssm_out), jnp.stack(conv_out),
            k_new.reshape(kv_shape), v_new.reshape(kv_shape))


def kernel(x_prompt, x_sample, p_prompt, p_sample, state_ssm, state_conv, cache_k, cache_v,
           ffn_norm, ffn_w_gate, ffn_w_up, ffn_w_down, mix_norm,
           ssm_w_in, ssm_conv_w, ssm_conv_b, ssm_dt_bias, ssm_a_log, ssm_d, ssm_norm, ssm_w_out,
           kv_norm, w_k, w_v, sb_w_q, sb_w_o,
           ple_norm, ple_w_gate, ple_w_proj, final_norm):
    depth, d = mix_norm.shape
    n_a = ssm_w_in.shape[0]
    n_b = sb_w_q.shape[0]
    head_order = _pair_major_order(sb_w_q.shape[2] // SB_HEAD_DIM)
    wts = {
        "ffn_norm": ffn_norm,
        "ffn": [[{"g": ffn_norm[i, s].reshape(1, d),
                  "wg": ffn_w_gate[i, s].astype(BF16),
                  "wu": ffn_w_up[i, s].astype(BF16),
                  "wd": ffn_w_down[i, s].astype(BF16)} for s in range(2)] for i in range(depth)],
        "mix_norm": [mix_norm[i].reshape(1, d) for i in range(depth)],
        "mamba": [_prepare_mamba(ssm_w_in[i], ssm_conv_w[i], ssm_conv_b[i], ssm_dt_bias[i],
                                 ssm_a_log[i], ssm_d[i], ssm_norm[i], ssm_w_out[i])
                  for i in range(n_a)],
        "kv": {"g": kv_norm.reshape(1, d), "wk": w_k.astype(BF16), "wv": w_v.astype(BF16)},
        "wq": [sb_w_q[j][:, head_order].astype(BF16) for j in range(n_b)],
        "wo": [sb_w_o[j][head_order, :].astype(BF16) for j in range(n_b)],
        "ple": [{"g": ple_norm[i].reshape(1, d),
                 "wgate": ple_w_gate[i].astype(BF16),
                 "wproj": ple_w_proj[i].astype(BF16)} for i in range(depth)],
        "final_norm": final_norm.reshape(1, d),
    }
    y_p, ssm_p, conv_p, k_p, v_p = _trunk(x_prompt, p_prompt, None, None, None, None, wts)
    y_s, ssm_s, conv_s, k_s, v_s = _trunk(x_sample, p_sample, state_ssm, state_conv,
                                          cache_k, cache_v, wts)
    return (y_p, y_s, ssm_p, conv_p, k_p, v_p, ssm_s, conv_s, k_s, v_s)
```

```python
import functools
import itertools

import jax
import jax.numpy as jnp
from jax import lax
from jax.experimental import pallas as pl
from jax.experimental.pallas import tpu as pltpu

F32 = jnp.float32
BF16 = jnp.bfloat16
NORM_EPS = 1e-6

SSM_HEAD_DIM = 64
SSM_GROUPS = 4
D_STATE = 128
CONV_W = 4
SSD_CHUNK = 64
SB_HEAD_DIM = 64
SB_KV_HEADS = 4
Q_BLOCK = 128
KEY_BLOCK = 128

VMEM_LIMIT_BYTES = 56 * 1024 * 1024
TOKEN_TILE = 512
MAMBA_TILE = 256
MAMBA_SEGMENT = 256
MXU_TILE = 256
CONV_PAD = 8


def _rms(x, g):
    return x * lax.rsqrt(jnp.mean(x * x, axis=-1, keepdims=True) + NORM_EPS) * g


def _sigmoid(v):
    return 1.0 / (1.0 + jnp.exp(-v))


def _silu(v):
    return v * _sigmoid(v)


def _softplus(v):
    return jnp.maximum(v, 0.0) + jnp.log(1.0 + jnp.exp(-jnp.abs(v)))


def _dot(a, b):
    return jnp.dot(a, b, preferred_element_type=F32)


def _dot_nt(a, b):
    return lax.dot_general(a, b, (((1,), (1,)), ((), ())), preferred_element_type=F32)


def _dot_tn(a, b):
    return lax.dot_general(a, b, (((0,), (0,)), ((), ())), preferred_element_type=F32)


def _dot_exact(a, b):
    return jnp.dot(a, b, precision=lax.Precision.HIGHEST, preferred_element_type=F32)


def _resident(shape):
    zeros = (0,) * len(shape)
    return pl.BlockSpec(shape, lambda *_: zeros, pipeline_mode=pl.Buffered(1))


def _params(n_axes):
    return pltpu.CompilerParams(dimension_semantics=("arbitrary",) * n_axes,
                                vmem_limit_bytes=VMEM_LIMIT_BYTES)


def _ffn_chunks(d_ff):
    if d_ff % MXU_TILE:
        return [(0, d_ff)]
    tiles = d_ff // MXU_TILE
    first = (tiles + 1) // 2 * MXU_TILE
    return [(0, first), (first, d_ff)] if first < d_ff else [(0, d_ff)]


def _token_body(*refs, with_ple, final, with_kv):
    it = iter(refs)
    x_ref, g_ref, wg_ref, wu_ref, wd_ref = (next(it) for _ in range(5))
    if with_ple:
        p_ref, pg_ref, wgate_ref, wproj_ref = (next(it) for _ in range(4))
    if final:
        fin_ref = next(it)
    if with_kv:
        kvg_ref, wk_ref, wv_ref = (next(it) for _ in range(3))
    o_ref = next(it)

    x = x_ref[...]
    h = _rms(x, g_ref[...]).astype(BF16)
    acc = jnp.zeros_like(x)
    for lo, hi in _ffn_chunks(wg_ref.shape[1]):
        gate = _dot(h, wg_ref[:, lo:hi])
        up = _dot(h, wu_ref[:, lo:hi])
        act = (_silu(gate) * up).astype(BF16)
        acc = acc + _dot(act, wd_ref[lo:hi, :])
    y = x + 0.5 * acc
    if with_ple:
        hp = _rms(y, pg_ref[...]).astype(BF16)
        y = y + _sigmoid(_dot(hp, wgate_ref[...])) * _dot(p_ref[...].astype(BF16), wproj_ref[...])
    if final:
        y = _rms(y, fin_ref[...])
    o_ref[...] = y
    if with_kv:
        k_ref, v_ref = next(it), next(it)
        hk = _rms(y, kvg_ref[...]).astype(BF16)
        k_ref[...] = _dot(hk, wk_ref[...])
        v_ref[...] = _dot(hk, wv_ref[...])


def _token_stage(x, ffn, ple=None, p=None, final_g=None, kv=None):
    t, d = x.shape
    tm = min(TOKEN_TILE, t)
    row = lambda n: pl.BlockSpec((tm, n), lambda i: (i, 0))
    args = [x, ffn["g"], ffn["wg"], ffn["wu"], ffn["wd"]]
    in_specs = [row(d)] + [_resident(a.shape) for a in args[1:]]
    if ple is not None:
        extra = [ple["g"], ple["wgate"], ple["wproj"]]
        args += [p] + extra
        in_specs += [row(p.shape[1])] + [_resident(a.shape) for a in extra]
    if final_g is not None:
        args.append(final_g)
        in_specs.append(_resident(final_g.shape))
    out_specs = [row(d)]
    out_shape = [jax.ShapeDtypeStruct((t, d), F32)]
    if kv is not None:
        extra = [kv["g"], kv["wk"], kv["wv"]]
        args += extra
        in_specs += [_resident(a.shape) for a in extra]
        dkv = kv["wk"].shape[1]
        out_specs += [row(dkv), row(dkv)]
        out_shape += [jax.ShapeDtypeStruct((t, dkv), F32)] * 2
    return pl.pallas_call(
        functools.partial(_token_body, with_ple=ple is not None, final=final_g is not None,
                          with_kv=kv is not None),
        grid=(t // tm,),
        in_specs=in_specs,
        out_specs=out_specs,
        out_shape=out_shape,
        compiler_params=_params(1),
        name="token_stage",
    )(*args)


def _kv_layout_body(k_ref, v_ref, kt_ref, vb_ref):
    tm = k_ref.shape[1]
    hd = SB_HEAD_DIM
    pw = 2 * hd
    kt = k_ref[0].T.astype(BF16)
    v = v_ref[0].astype(BF16)
    first_head = lax.broadcasted_iota(jnp.int32, (KEY_BLOCK, pw), 1) < hd
    zeros_k = jnp.zeros((hd, KEY_BLOCK), BF16)
    zeros_v = jnp.zeros((KEY_BLOCK, pw), BF16)
    for pr in range(SB_KV_HEADS // 2):
        for kb in range(tm // KEY_BLOCK):
            keys = slice(kb * KEY_BLOCK, (kb + 1) * KEY_BLOCK)
            kt_ref[0, pr, kb, 0:hd, 0:KEY_BLOCK] = kt[(2 * pr) * hd:(2 * pr + 1) * hd, keys]
            kt_ref[0, pr, kb, 0:hd, KEY_BLOCK:2 * KEY_BLOCK] = zeros_k
            kt_ref[0, pr, kb, hd:pw, 0:KEY_BLOCK] = zeros_k
            kt_ref[0, pr, kb, hd:pw, KEY_BLOCK:2 * KEY_BLOCK] = kt[(2 * pr + 1) * hd:(2 * pr + 2) * hd, keys]
            vp = v[keys, pr * pw:(pr + 1) * pw]
            vb_ref[0, pr, kb, 0:KEY_BLOCK, :] = jnp.where(first_head, vp, zeros_v)
            vb_ref[0, pr, kb, KEY_BLOCK:2 * KEY_BLOCK, :] = jnp.where(first_head, zeros_v, vp)


def _kv_layout(k, v):
    b, nk, dkv = k.shape
    tm = min(TOKEN_TILE, nk)
    n_pairs = SB_KV_HEADS // 2
    pw = 2 * SB_HEAD_DIM
    nb = tm // KEY_BLOCK
    return pl.pallas_call(
        _kv_layout_body,
        grid=(b, nk // tm),
        in_specs=[pl.BlockSpec((1, tm, dkv), lambda i, j: (i, j, 0)),
                  pl.BlockSpec((1, tm, dkv), lambda i, j: (i, j, 0))],
        out_specs=[pl.BlockSpec((1, n_pairs, nb, pw, 2 * KEY_BLOCK), lambda i, j: (i, 0, j, 0, 0)),
                   pl.BlockSpec((1, n_pairs, nb, 2 * KEY_BLOCK, pw), lambda i, j: (i, 0, j, 0, 0))],
        out_shape=[jax.ShapeDtypeStruct((b, n_pairs, nk // KEY_BLOCK, pw, 2 * KEY_BLOCK), BF16),
                   jax.ShapeDtypeStruct((b, n_pairs, nk // KEY_BLOCK, 2 * KEY_BLOCK, pw), BF16)],
        compiler_params=_params(2),
        name="kv_layout",
    )(k, v)


ATTN_EXIT = 104.0


def _attn_body(x_ref, g_ref, wq_ref, wo_ref, kd_ref, vd_ref, kp_ref, vp_ref, *refs,
               diag_block, blocks_before, report_swept):
    if report_swept:
        o_ref, swept_ref, q_ref, acc_ref, carry_ref = refs
    else:
        o_ref, q_ref, acc_ref, carry_ref = refs
    qb = x_ref.shape[1]
    n_pairs = kd_ref.shape[1]
    pw = 2 * SB_HEAD_DIM
    tile = 2 * KEY_BLOCK
    q_per_kv = wq_ref.shape[1] // (SB_KV_HEADS * SB_HEAD_DIM)
    rows = q_per_kv * qb
    qi = pl.program_id(1)
    d_blk = qi if diag_block is None else diag_block
    last_past = (qi if blocks_before is None else jnp.int32(blocks_before)) - 1

    x = x_ref[0]
    h = _rms(x, g_ref[...]).astype(BF16)
    q = (_dot(h, wq_ref[...]) * (SB_HEAD_DIM ** -0.5)).astype(BF16)
    for pr in range(n_pairs):
        for g in range(q_per_kv):
            c0 = (pr * q_per_kv + g) * pw
            q_ref[pr, g * qb:(g + 1) * qb, :] = q[:, c0:c0 + pw]

    key_i = lax.broadcasted_iota(jnp.int32, (tile, tile), 0)
    key_s = lax.broadcasted_iota(jnp.int32, (tile, tile), 1)
    same_head = (key_i >= KEY_BLOCK) == (key_s >= KEY_BLOCK)
    suffix_ones = ((key_i >= key_s) & same_head).astype(BF16)
    row_q = lax.broadcasted_iota(jnp.int32, (rows, tile), 0) % qb
    col_k = lax.broadcasted_iota(jnp.int32, (rows, tile), 1) % KEY_BLOCK
    causal = col_k < row_q

    def sweep(pr, kt, vb, first):
        z = _dot(q_ref[pr], kt)
        sp = _softplus(z)
        if first:
            sp = jnp.where(causal, sp, 0.0)
        cs = _dot(sp.astype(BF16), suffix_ones)
        log_a = z - cs
        if first:
            log_a = jnp.where(causal, log_a, -jnp.inf)
        else:
            log_a = log_a - carry_ref[pr]
        pv = _dot(jnp.exp(log_a).astype(BF16), vb)
        total = jnp.concatenate(
            [jnp.broadcast_to(cs[:, 0:1], (rows, KEY_BLOCK)),
             jnp.broadcast_to(cs[:, KEY_BLOCK:KEY_BLOCK + 1], (rows, KEY_BLOCK))], axis=1)
        if first:
            acc_ref[pr] = pv
            carry_ref[pr] = total
        else:
            acc_ref[pr] += pv
            carry_ref[pr] += total

    for pr in range(n_pairs):
        sweep(pr, kd_ref[0, pr, d_blk], vd_ref[0, pr, d_blk], True)

    def more(state):
        kb, swept = state
        return jnp.logical_and(kb >= 0, swept < ATTN_EXIT)

    def step(state):
        kb, _ = state
        for pr in range(n_pairs):
            sweep(pr, kp_ref[0, pr, kb], vp_ref[0, pr, kb], False)
        return kb - 1, jnp.min(carry_ref[...])

    _, swept = lax.while_loop(more, step, (last_past, jnp.min(carry_ref[...])))
    if report_swept:
        swept_ref[0, 0] = jnp.full(swept_ref.shape[2:], swept, F32)

    o = jnp.concatenate([acc_ref[pr, g * qb:(g + 1) * qb, :]
                         for pr in range(n_pairs) for g in range(q_per_kv)], axis=1)
    o_ref[0] = x + _dot(o.astype(BF16), wo_ref[...])


def _attn(x, g, wq, wo, kd, vd, kp, vp, diag_block, blocks_before, report_swept=False):
    b, l, d = x.shape
    dq = wq.shape[1]
    qb = min(Q_BLOCK, l)
    n_pairs = SB_KV_HEADS // 2
    pw = 2 * SB_HEAD_DIM
    rows = (dq // (SB_KV_HEADS * SB_HEAD_DIM)) * qb

    def whole(a):
        return pl.BlockSpec((1,) + a.shape[1:], lambda i, j: (i, 0, 0, 0, 0))

    out_specs = [pl.BlockSpec((1, qb, d), lambda i, j: (i, j, 0))]
    out_shape = [jax.ShapeDtypeStruct((b, l, d), F32)]
    if report_swept:
        out_specs.append(pl.BlockSpec((1, 1, 8, 128), lambda i, j: (i, j, 0, 0)))
        out_shape.append(jax.ShapeDtypeStruct((b, l // qb, 8, 128), F32))
    return pl.pallas_call(
        functools.partial(_attn_body, diag_block=diag_block, blocks_before=blocks_before,
                          report_swept=report_swept),
        grid=(b, l // qb),
        in_specs=[pl.BlockSpec((1, qb, d), lambda i, j: (i, j, 0)),
                  _resident((1, d)), _resident((d, dq)), _resident((dq, d)),
                  whole(kd), whole(vd), whole(kp), whole(vp)],
        out_specs=out_specs,
        out_shape=out_shape,
        scratch_shapes=[pltpu.VMEM((n_pairs, rows, pw), BF16),
                        pltpu.VMEM((n_pairs, rows, pw), F32),
                        pltpu.VMEM((n_pairs, rows, 2 * KEY_BLOCK), F32)],
        compiler_params=_params(2),
        name="sb_attn",
    )(x, g, wq, wo, kd, vd, kp, vp)


def _pair_major_order(n_heads):
    q_per_kv = n_heads // SB_KV_HEADS
    order = []
    for pr in range(SB_KV_HEADS // 2):
        for g in range(q_per_kv):
            for s in range(2):
                head = (2 * pr + s) * q_per_kv + g
                order.extend(range(head * SB_HEAD_DIM, (head + 1) * SB_HEAD_DIM))
    return jnp.asarray(order, jnp.int32)


def _mamba_stage1(x_ref, g_ref, wz_ref, wx_ref, wdt_ref, wdtt_ref, cw_ref, cb_ref, dtb_ref,
                  dtbt_ref, pad_ref, stage, *, valid):
    xs_ref, bc_ref, zs_ref, dt_ref, dtt_ref = stage
    tt = x_ref.shape[1]
    d_inner = wz_ref.shape[1]
    conv_dim = wx_ref.shape[1]
    seg = MAMBA_SEGMENT
    h = _rms(x_ref[0], g_ref[...]).astype(BF16)

    def dt_piece():
        dt = _softplus(_dot(h, wdt_ref[...]) + dtb_ref[...])
        dtt = _softplus(_dot_nt(wdtt_ref[...], h) + dtbt_ref[...])
        if valid < tt:
            dt = jnp.where(lax.broadcasted_iota(jnp.int32, dt.shape, 0) < valid, dt, 0.0)
            dtt = jnp.where(lax.broadcasted_iota(jnp.int32, dtt.shape, 1) < valid, dtt, 0.0)
        dt_ref[...] = dt
        dtt_ref[...] = dtt

    def conv_piece(lo):
        cols = slice(lo, lo + seg)
        u = _dot(h, wx_ref[:, cols])
        pad_ref[CONV_PAD:CONV_PAD + tt, cols] = u
        conv = cb_ref[:, cols] + u * cw_ref[CONV_W - 1:CONV_W, cols]
        for k in range(CONV_W - 2, -1, -1):
            off = CONV_PAD - (CONV_W - 1) + k
            conv = conv + pad_ref[off:off + tt, cols] * cw_ref[k:k + 1, cols]
        act = _silu(conv)
        if lo < d_inner:
            xs_ref[:, cols] = act
        else:
            bc_ref[:, lo - d_inner:lo - d_inner + seg] = act.astype(BF16)

    def gate_piece(lo):
        cols = slice(lo, lo + seg)
        zs_ref[:, cols] = _silu(_dot(h, wz_ref[:, cols]))

    assert d_inner % seg == 0 and conv_dim % seg == 0
    return ([dt_piece]
            + [functools.partial(conv_piece, lo) for lo in range(0, conv_dim, seg)]
            + [functools.partial(gate_piece, lo) for lo in range(0, d_inner, seg)])


def _mamba_stage2(x_ref, alog_ref, alogt_ref, dskip_ref, ng_ref, wout_ref, o_ref,
                  state_ref, y_ref, xbd_ref, stage, *, chunk):
    xs_ref, bc_ref, zs_ref, dt_ref, dtt_ref = stage
    tt = x_ref.shape[1]
    n_heads = dt_ref.shape[1]
    hpg = n_heads // SSM_GROUPS
    gw = hpg * SSM_HEAD_DIM
    gn = SSM_GROUPS * D_STATE
    xs = xs_ref[...]
    bm = bc_ref[:, :gn]
    cm = bc_ref[:, gn:]
    dt = dt_ref[...]
    dtt = dtt_ref[...]
    a = dt * (-jnp.exp(alog_ref[...]))
    at = dtt * (-jnp.exp(alogt_ref[...]))

    hd = SSM_HEAD_DIM
    pw, qw = 2 * hd, 4 * hd
    half = n_heads // 2
    ci = lax.broadcasted_iota(jnp.int32, (chunk, chunk), 0)
    cj = lax.broadcasted_iota(jnp.int32, (chunk, chunk), 1)
    lower_f = (ci >= cj).astype(F32)
    upper_f = (ci <= cj).astype(F32)
    lane_p = lax.broadcasted_iota(jnp.int32, (chunk, pw), 1)
    first = lane_p < hd
    lower2 = lax.broadcasted_iota(jnp.int32, (chunk, pw), 0) >= lane_p % hd

    def pair_cols(v, p):
        return jnp.where(first, v[:, 2 * p:2 * p + 1], v[:, 2 * p + 1:2 * p + 2])

    xs_bf = xs.astype(BF16)
    quads_per_group = hpg // 4

    def chunk_steps(c):
        rows = slice(c * chunk, (c + 1) * chunk)
        acs = _dot_exact(lower_f, a[rows, :])
        acst = _dot_exact(at[:, rows], upper_f)
        acst_p = jnp.concatenate([acst[:half], acst[half:]], axis=1)
        dtt_p = jnp.concatenate([dtt[:half, rows], dtt[half:, rows]], axis=1)
        dw = dt[rows, :] * jnp.exp(acs[chunk - 1:chunk, :] - acs)
        for g in range(SSM_GROUPS):
            b_c = bm[rows, g * D_STATE:(g + 1) * D_STATE]
            c_c = cm[rows, g * D_STATE:(g + 1) * D_STATE]
            cb2 = _dot_nt(c_c, jnp.concatenate([b_c, b_c], axis=0))
            s_g = state_ref[g]
            y_off = _dot(c_c, s_g.astype(BF16))
            xw, dec = [], []
            for q in range(quads_per_group):
                quad = g * quads_per_group + q
                lo = quad * qw
                m_q, e_q = [], []
                for p in (2 * quad, 2 * quad + 1):
                    col = pair_cols(acs, p)
                    seg = col - acst_p[p:p + 1, :]
                    decay = jnp.exp(jnp.where(lower2, seg, -jnp.inf))
                    m_q.append((cb2 * decay * dtt_p[p:p + 1, :]).astype(BF16))
                    e_q.append(jnp.exp(col))
                    dec.append(jnp.exp(col[chunk - 1:chunk, :]))
                    xw.append((xs[rows, p * pw:(p + 1) * pw] * pair_cols(dw, p)).astype(BF16))
                slot = c * (n_heads // 4) + quad
                for k in range(4):
                    xbd_ref[slot, k * chunk:(k + 1) * chunk, k * hd:(k + 1) * hd] = (
                        xs_bf[rows, lo + k * hd:lo + (k + 1) * hd])
                y_q = _dot(jnp.concatenate(m_q, axis=1), xbd_ref[slot])
                y_q = y_q + y_off[:, q * qw:(q + 1) * qw] * jnp.concatenate(e_q, axis=1)
                y_ref[rows, lo:lo + qw] = y_q
            state_ref[g] = (s_g * jnp.concatenate(dec, axis=1)
                            + _dot_tn(b_c, jnp.concatenate(xw, axis=1)))
            yield

    def finish():
        y = y_ref[...] + xs * dskip_ref[...]
        y = y * zs_ref[...]
        normed = []
        for g in range(SSM_GROUPS):
            yg = y[:, g * gw:(g + 1) * gw]
            normed.append(yg * lax.rsqrt(jnp.mean(yg * yg, axis=-1, keepdims=True) + NORM_EPS))
        y = jnp.concatenate(normed, axis=1) * ng_ref[...]
        o_ref[0] = x_ref[0] + _dot(y.astype(BF16), wout_ref[...])

    return [chunk_steps(c) for c in range(tt // chunk)], finish


def _mamba_body(xa_ref, xb_ref, g_ref, wz_ref, wx_ref, wdt_ref, wdtt_ref, cw_ref, cb_ref,
                dtb_ref, dtbt_ref, alog_ref, alogt_ref, dskip_ref, ng_ref, wout_ref,
                s0_ref, c0_ref,
                o_ref, sfin_ref, cfin_ref,
                state_ref, pad_ref, y_ref, xbd_ref, *stages,
                tiles_per_stream, n_tiles, chunk, valid, has_init):
    tt = xa_ref.shape[1]
    n_stage = len(stages) // 2
    stage_sets = (stages[:n_stage], stages[n_stage:])
    s = pl.program_id(0)
    tile1 = jnp.minimum(s, n_tiles - 1)
    tile2 = jnp.maximum(s - 1, 0)
    pos1 = lax.rem(tile1, tiles_per_stream)
    pos2 = lax.rem(tile2, tiles_per_stream)

    @pl.when(s == 0)
    def _first_step():
        for ref in stage_sets[1]:
            ref[...] = jnp.zeros(ref.shape, ref.dtype)
        xbd_ref[...] = jnp.zeros(xbd_ref.shape, BF16)

    @pl.when(pos1 == 0)
    def _new_stream_conv():
        pad_ref[0:CONV_PAD, :] = jnp.zeros((CONV_PAD, pad_ref.shape[1]), F32)
        if has_init:
            pad_ref[CONV_PAD - (CONV_W - 1):CONV_PAD, :] = c0_ref[0]

    @pl.when(pos2 == 0)
    def _new_stream_state():
        if has_init:
            for g in range(SSM_GROUPS):
                state_ref[g] = s0_ref[0, g].T
        else:
            state_ref[...] = jnp.zeros(state_ref.shape, F32)

    def step(write, read):
        first = _mamba_stage1(xa_ref, g_ref, wz_ref, wx_ref, wdt_ref, wdtt_ref, cw_ref, cb_ref,
                              dtb_ref, dtbt_ref, pad_ref, write, valid=valid)
        chunks, finish = _mamba_stage2(xb_ref, alog_ref, alogt_ref, dskip_ref, ng_ref, wout_ref,
                                       o_ref, state_ref, y_ref, xbd_ref, read, chunk=chunk)
        per_step = -(-len(first) // (len(chunks) * SSM_GROUPS))
        pending = iter(first)
        for steps in chunks:
            for _ in steps:
                for piece in itertools.islice(pending, per_step):
                    piece()
        for piece in pending:
            piece()
        finish()

    parity = lax.rem(s, 2)
    pl.when(parity == 0)(lambda: step(stage_sets[0], stage_sets[1]))
    pl.when(parity == 1)(lambda: step(stage_sets[1], stage_sets[0]))

    @pl.when(jnp.logical_and(s < n_tiles, pos1 == tiles_per_stream - 1))
    def _conv_out():
        cfin_ref[0] = pad_ref[CONV_PAD + valid - (CONV_W - 1):CONV_PAD + valid, :]

    @pl.when(jnp.logical_and(s >= 1, pos2 == tiles_per_stream - 1))
    def _state_out():
        for g in range(SSM_GROUPS):
            sfin_ref[0, g] = state_ref[g].T

    pad_ref[0:CONV_PAD, :] = pad_ref[tt:tt + CONV_PAD, :]


def _mamba(x, g, w, s0, c0):
    b, l, d = x.shape
    d_inner = w["wz"].shape[1]
    conv_dim = w["wx"].shape[1]
    n_heads = w["wdt"].shape[1]
    gw = (n_heads // SSM_GROUPS) * SSM_HEAD_DIM
    chunk = SSD_CHUNK
    assert chunk == SSM_HEAD_DIM and (n_heads // SSM_GROUPS) % 4 == 0
    valid = l
    if l < chunk:
        x = jnp.pad(x, ((0, 0), (0, chunk - l), (0, 0)))
    l_pad = x.shape[1]
    tt = min(MAMBA_TILE, l_pad)
    assert l_pad % tt == 0 and tt % chunk == 0
    if l_pad == l:
        valid = tt
    has_init = s0 is not None
    nt = l_pad // tt
    n_tiles = b * nt

    def tile1(s):
        return jnp.minimum(s, n_tiles - 1)

    def tile2(s):
        return jnp.maximum(s - 1, 0)

    if not has_init:
        s0 = jnp.zeros((1, SSM_GROUPS, gw, D_STATE), F32)
        c0 = jnp.zeros((1, CONV_W - 1, conv_dim), F32)
        init_map = lambda s: (0, 0, 0, 0)
        conv_map = lambda s: (0, 0, 0)
    else:
        init_map = lambda s: (tile2(s) // nt, 0, 0, 0)
        conv_map = lambda s: (tile1(s) // nt, 0, 0)
    n_slots = (tt // chunk) * (n_heads // 4)
    staging = [pltpu.VMEM((tt, d_inner), F32),
               pltpu.VMEM((tt, conv_dim - d_inner), BF16),
               pltpu.VMEM((tt, d_inner), F32),
               pltpu.VMEM((tt, n_heads), F32),
               pltpu.VMEM((n_heads, tt), F32)]
    out, s_fin, c_fin = pl.pallas_call(
        functools.partial(_mamba_body, tiles_per_stream=nt, n_tiles=n_tiles, chunk=chunk,
                          valid=valid, has_init=has_init),
        grid=(n_tiles + 1,),
        in_specs=[pl.BlockSpec((1, tt, d), lambda s: (tile1(s) // nt, tile1(s) % nt, 0)),
                  pl.BlockSpec((1, tt, d), lambda s: (tile2(s) // nt, tile2(s) % nt, 0)),
                  _resident((1, d)), _resident((d, d_inner)), _resident((d, conv_dim)),
                  _resident((d, n_heads)), _resident((n_heads, d)),
                  _resident((CONV_W, conv_dim)), _resident((1, conv_dim)),
                  _resident((1, n_heads)), _resident((n_heads, 1)),
                  _resident((1, n_heads)), _resident((n_heads, 1)),
                  _resident((1, d_inner)), _resident((1, d_inner)), _resident((d_inner, d)),
                  pl.BlockSpec((1, SSM_GROUPS, gw, D_STATE), init_map),
                  pl.BlockSpec((1, CONV_W - 1, conv_dim), conv_map)],
        out_specs=[pl.BlockSpec((1, tt, d), lambda s: (tile2(s) // nt, tile2(s) % nt, 0)),
                   pl.BlockSpec((1, SSM_GROUPS, gw, D_STATE), lambda s: (tile2(s) // nt, 0, 0, 0)),
                   pl.BlockSpec((1, CONV_W - 1, conv_dim), lambda s: (tile1(s) // nt, 0, 0))],
        out_shape=[jax.ShapeDtypeStruct((b, l_pad, d), F32),
                   jax.ShapeDtypeStruct((b, SSM_GROUPS, gw, D_STATE), F32),
                   jax.ShapeDtypeStruct((b, CONV_W - 1, conv_dim), F32)],
        scratch_shapes=[pltpu.VMEM((SSM_GROUPS, D_STATE, gw), F32),
                        pltpu.VMEM((CONV_PAD + tt, conv_dim), F32),
                        pltpu.VMEM((tt, d_inner), F32),
                        pltpu.VMEM((n_slots, 4 * chunk, 4 * SSM_HEAD_DIM), BF16)] + staging + staging,
        compiler_params=_params(1),
        name="mamba2",
    )(x, x, g, w["wz"], w["wx"], w["wdt"], w["wdtt"], w["conv_w"], w["conv_b"],
      w["dt_bias"], w["dt_bias_t"], w["a_log"], w["a_log_t"], w["d_skip"], w["norm_g"],
      w["wout"], s0, c0)
    return out[:, :l], s_fin, c_fin


def _prepare_mamba(w_in, conv_w, conv_b, dt_bias, a_log, d_skip, norm_g, w_out):
    d_inner = w_out.shape[0]
    n_heads = dt_bias.shape[0]
    conv_dim = conv_w.shape[1]
    wdt = w_in[:, d_inner + conv_dim:].astype(BF16)
    eo = jnp.concatenate([jnp.arange(0, n_heads, 2), jnp.arange(1, n_heads, 2)])
    return {
        "wz": w_in[:, :d_inner].astype(BF16),
        "wx": w_in[:, d_inner:d_inner + conv_dim].astype(BF16),
        "wdt": wdt,
        "wdtt": wdt.T[eo],
        "conv_w": conv_w,
        "conv_b": conv_b.reshape(1, conv_dim),
        "dt_bias": dt_bias.reshape(1, n_heads),
        "dt_bias_t": dt_bias[eo].reshape(n_heads, 1),
        "a_log": a_log.reshape(1, n_heads),
        "a_log_t": a_log[eo].reshape(n_heads, 1),
        "d_skip": jnp.repeat(d_skip, SSM_HEAD_DIM).reshape(1, d_inner),
        "norm_g": norm_g.reshape(1, d_inner),
        "wout": w_out.astype(BF16),
    }


NEAR_PAST_BLOCKS = 4


def _attn_with_past(layer, k_past, v_past):
    b, past = k_past.shape[:2]

    def sweep(n_keys, report):
        kp, vp = _kv_layout(k_past[:, past - n_keys:].reshape(b, n_keys, -1),
                            v_past[:, past - n_keys:].reshape(b, n_keys, -1))
        return _attn(*layer, kp, vp, 0, n_keys // KEY_BLOCK, report_swept=report)

    near = min(NEAR_PAST_BLOCKS * KEY_BLOCK, past)
    if near == past:
        return sweep(past, False)[0]
    out_near, swept = sweep(near, True)
    return lax.cond(jnp.min(swept) >= ATTN_EXIT, lambda: out_near, lambda: sweep(past, False)[0])


def _trunk(x, p, ssm0, conv0, k_past, v_past, wts):
    b, l, d = x.shape
    t = b * l
    depth = wts["ffn_norm"].shape[0]
    n_a = len(wts["mamba"])
    assert 1 <= n_a < depth
    ssm_out, conv_out = [], []
    k_new = v_new = kd = vd = None
    xf = x.reshape(t, d)
    for i in range(depth):
        xf, = _token_stage(xf, wts["ffn"][i][0])
        if i < n_a:
            s0 = None if ssm0 is None else ssm0[i].reshape(b, SSM_GROUPS, -1, D_STATE)
            c0 = None if conv0 is None else conv0[i]
            x3, s_fin, c_fin = _mamba(xf.reshape(b, l, d), wts["mix_norm"][i], wts["mamba"][i], s0, c0)
            xf = x3.reshape(t, d)
            ssm_out.append(s_fin.reshape(b, -1, SSM_HEAD_DIM, D_STATE))
            conv_out.append(c_fin)
        else:
            jb = i - n_a
            layer = (xf.reshape(b, l, d), wts["mix_norm"][i], wts["wq"][jb], wts["wo"][jb], kd, vd)
            if k_past is None:
                x3, = _attn(*layer, kd, vd, None, None)
            else:
                x3 = _attn_with_past(layer, k_past, v_past)
            xf = x3.reshape(t, d)
        outs = _token_stage(xf, wts["ffn"][i][1], ple=wts["ple"][i], p=p[i].reshape(t, -1),
                            final_g=wts["final_norm"] if i == depth - 1 else None,
                            kv=wts["kv"] if i + 1 == n_a else None)
        xf = outs[0]
        if i + 1 == n_a:
            k_new, v_new = outs[1], outs[2]
            k3 = k_new.reshape(b, l, -1)
            v3 = v_new.reshape(b, l, -1)
            if k_past is None:
                kd, vd = _kv_layout(k3, v3)
            else:
                assert k_past.shape[1] % KEY_BLOCK == 0 and l <= KEY_BLOCK
                fill = ((0, 0), (0, KEY_BLOCK - l), (0, 0))
                kd, vd = _kv_layout(jnp.pad(k3, fill), jnp.pad(v3, fill))
    kv_shape = (b, l, SB_KV_HEADS, SB_HEAD_DIM)
    return (xf.reshape(b, l, d), jnp.stack(ssm_out), jnp.stack(conv_out),
            k_new.reshape(kv_shape), v_new.reshape(kv_shape))


def kernel(x_prompt, x_sample, p_prompt, p_sample, state_ssm, state_conv, cache_k, cache_v,
           ffn_norm, ffn_w_gate, ffn_w_up, ffn_w_down, mix_norm,
           ssm_w_in, ssm_conv_w, ssm_conv_b, ssm_dt_bias, ssm_a_log, ssm_d, ssm_norm, ssm_w_out,
           kv_norm, w_k, w_v, sb_w_q, sb_w_o,
           ple_norm, ple_w_gate, ple_w_proj, final_norm):
    depth, d = mix_norm.shape
    n_a = ssm_w_in.shape[0]
    n_b = sb_w_q.shape[0]
    head_order = _pair_major_order(sb_w_q.shape[2] // SB_HEAD_DIM)
    wts = {
        "ffn_norm": ffn_norm,
        "ffn": [[{"g": ffn_norm[i, s].reshape(1, d),
                  "wg": ffn_w_gate[i, s].astype(BF16),
                  "wu": ffn_w_up[i, s].astype(BF16),
                  "wd": ffn_w_down[i, s].astype(BF16)} for s in range(2)] for i in range(depth)],
        "mix_norm": [mix_norm[i].reshape(1, d) for i in range(depth)],
        "mamba": [_prepare_mamba(ssm_w_in[i], ssm_conv_w[i], ssm_conv_b[i], ssm_dt_bias[i],
                                 ssm_a_log[i], ssm_d[i], ssm_norm[i], ssm_w_out[i])
                  for i in range(n_a)],
        "kv": {"g": kv_norm.reshape(1, d), "wk": w_k.astype(BF16), "wv": w_v.astype(BF16)},
        "wq": [sb_w_q[j][:, head_order].astype(BF16) for j in range(n_b)],
        "wo": [sb_w_o[j][head_order, :].astype(BF16) for j in range(n_b)],
        "ple": [{"g": ple_norm[i].reshape(1, d),
                 "wgate": ple_w_gate[i].astype(BF16),
                 "wproj": ple_w_proj[i].astype(BF16)} for i in range(depth)],
        "final_norm": final_norm.reshape(1, d),
    }
    y_p, ssm_p, conv_p, k_p, v_p = _trunk(x_prompt, p_prompt, None, None, None, None, wts)
    y_s, ssm_s, conv_s, k_s, v_s = _trunk(x_sample, p_sample, state_ssm, state_conv,
                                          cache_k, cache_v, wts)
    return (y_p, y_s, ssm_p, conv_p, k_p, v_p, ssm_s, conv_s, k_s, v_s)
```

```python
import functools

import jax
import jax.numpy as jnp
from jax import lax
from jax.experimental import pallas as pl
from jax.experimental.pallas import tpu as pltpu

F32 = jnp.float32
BF16 = jnp.bfloat16
NORM_EPS = 1e-6

SSM_HEAD_DIM = 64
SSM_GROUPS = 4
D_STATE = 128
CONV_W = 4
SSD_CHUNK = 64
SB_HEAD_DIM = 64
SB_KV_HEADS = 4
Q_BLOCK = 128
KEY_BLOCK = 128

VMEM_LIMIT_BYTES = 56 * 1024 * 1024
TOKEN_TILE = 512
MAMBA_TILE = 512
MXU_TILE = 256
CONV_PAD = 8


def _rms(x, g):
    return x * lax.rsqrt(jnp.mean(x * x, axis=-1, keepdims=True) + NORM_EPS) * g


def _sigmoid(v):
    return 1.0 / (1.0 + jnp.exp(-v))


def _silu(v):
    return v * _sigmoid(v)


def _softplus(v):
    return jnp.maximum(v, 0.0) + jnp.log(1.0 + jnp.exp(-jnp.abs(v)))


def _dot(a, b):
    return jnp.dot(a, b, preferred_element_type=F32)


def _dot_nt(a, b):
    return lax.dot_general(a, b, (((1,), (1,)), ((), ())), preferred_element_type=F32)


def _dot_tn(a, b):
    return lax.dot_general(a, b, (((0,), (0,)), ((), ())), preferred_element_type=F32)


def _dot_exact(a, b):
    return jnp.dot(a, b, precision=lax.Precision.HIGHEST, preferred_element_type=F32)


def _resident(shape):
    zeros = (0,) * len(shape)
    return pl.BlockSpec(shape, lambda *_: zeros, pipeline_mode=pl.Buffered(1))


def _params(n_axes):
    return pltpu.CompilerParams(dimension_semantics=("arbitrary",) * n_axes,
                                vmem_limit_bytes=VMEM_LIMIT_BYTES)


def _ffn_chunks(d_ff):
    if d_ff % MXU_TILE:
        return [(0, d_ff)]
    tiles = d_ff // MXU_TILE
    first = (tiles + 1) // 2 * MXU_TILE
    return [(0, first), (first, d_ff)] if first < d_ff else [(0, d_ff)]


def _token_body(*refs, with_ple, final, with_kv):
    it = iter(refs)
    x_ref, g_ref, wg_ref, wu_ref, wd_ref = (next(it) for _ in range(5))
    if with_ple:
        p_ref, pg_ref, wgate_ref, wproj_ref = (next(it) for _ in range(4))
    if final:
        fin_ref = next(it)
    if with_kv:
        kvg_ref, wk_ref, wv_ref = (next(it) for _ in range(3))
    o_ref = next(it)

    x = x_ref[...]
    h = _rms(x, g_ref[...]).astype(BF16)
    acc = jnp.zeros_like(x)
    for lo, hi in _ffn_chunks(wg_ref.shape[1]):
        gate = _dot(h, wg_ref[:, lo:hi])
        up = _dot(h, wu_ref[:, lo:hi])
        act = (_silu(gate) * up).astype(BF16)
        acc = acc + _dot(act, wd_ref[lo:hi, :])
    y = x + 0.5 * acc
    if with_ple:
        hp = _rms(y, pg_ref[...]).astype(BF16)
        y = y + _sigmoid(_dot(hp, wgate_ref[...])) * _dot(p_ref[...].astype(BF16), wproj_ref[...])
    if final:
        y = _rms(y, fin_ref[...])
    o_ref[...] = y
    if with_kv:
        k_ref, v_ref = next(it), next(it)
        hk = _rms(y, kvg_ref[...]).astype(BF16)
        k_ref[...] = _dot(hk, wk_ref[...])
        v_ref[...] = _dot(hk, wv_ref[...])


def _token_stage(x, ffn, ple=None, p=None, final_g=None, kv=None):
    t, d = x.shape
    tm = min(TOKEN_TILE, t)
    row = lambda n: pl.BlockSpec((tm, n), lambda i: (i, 0))
    args = [x, ffn["g"], ffn["wg"], ffn["wu"], ffn["wd"]]
    in_specs = [row(d)] + [_resident(a.shape) for a in args[1:]]
    if ple is not None:
        extra = [ple["g"], ple["wgate"], ple["wproj"]]
        args += [p] + extra
        in_specs += [row(p.shape[1])] + [_resident(a.shape) for a in extra]
    if final_g is not None:
        args.append(final_g)
        in_specs.append(_resident(final_g.shape))
    out_specs = [row(d)]
    out_shape = [jax.ShapeDtypeStruct((t, d), F32)]
    if kv is not None:
        extra = [kv["g"], kv["wk"], kv["wv"]]
        args += extra
        in_specs += [_resident(a.shape) for a in extra]
        dkv = kv["wk"].shape[1]
        out_specs += [row(dkv), row(dkv)]
        out_shape += [jax.ShapeDtypeStruct((t, dkv), F32)] * 2
    return pl.pallas_call(
        functools.partial(_token_body, with_ple=ple is not None, final=final_g is not None,
                          with_kv=kv is not None),
        grid=(t // tm,),
        in_specs=in_specs,
        out_specs=out_specs,
        out_shape=out_shape,
        compiler_params=_params(1),
        name="token_stage",
    )(*args)


def _kv_layout_body(k_ref, v_ref, kt_ref, vb_ref):
    tm = k_ref.shape[1]
    hd = SB_HEAD_DIM
    pw = 2 * hd
    kt = k_ref[0].T.astype(BF16)
    v = v_ref[0].astype(BF16)
    first_head = lax.broadcasted_iota(jnp.int32, (KEY_BLOCK, pw), 1) < hd
    zeros_k = jnp.zeros((hd, KEY_BLOCK), BF16)
    zeros_v = jnp.zeros((KEY_BLOCK, pw), BF16)
    for pr in range(SB_KV_HEADS // 2):
        for kb in range(tm // KEY_BLOCK):
            keys = slice(kb * KEY_BLOCK, (kb + 1) * KEY_BLOCK)
            kt_ref[0, pr, kb, 0:hd, 0:KEY_BLOCK] = kt[(2 * pr) * hd:(2 * pr + 1) * hd, keys]
            kt_ref[0, pr, kb, 0:hd, KEY_BLOCK:2 * KEY_BLOCK] = zeros_k
            kt_ref[0, pr, kb, hd:pw, 0:KEY_BLOCK] = zeros_k
            kt_ref[0, pr, kb, hd:pw, KEY_BLOCK:2 * KEY_BLOCK] = kt[(2 * pr + 1) * hd:(2 * pr + 2) * hd, keys]
            vp = v[keys, pr * pw:(pr + 1) * pw]
            vb_ref[0, pr, kb, 0:KEY_BLOCK, :] = jnp.where(first_head, vp, zeros_v)
            vb_ref[0, pr, kb, KEY_BLOCK:2 * KEY_BLOCK, :] = jnp.where(first_head, zeros_v, vp)


def _kv_layout(k, v):
    b, nk, dkv = k.shape
    tm = min(TOKEN_TILE, nk)
    n_pairs = SB_KV_HEADS // 2
    pw = 2 * SB_HEAD_DIM
    nb = tm // KEY_BLOCK
    return pl.pallas_call(
        _kv_layout_body,
        grid=(b, nk // tm),
        in_specs=[pl.BlockSpec((1, tm, dkv), lambda i, j: (i, j, 0)),
                  pl.BlockSpec((1, tm, dkv), lambda i, j: (i, j, 0))],
        out_specs=[pl.BlockSpec((1, n_pairs, nb, pw, 2 * KEY_BLOCK), lambda i, j: (i, 0, j, 0, 0)),
                   pl.BlockSpec((1, n_pairs, nb, 2 * KEY_BLOCK, pw), lambda i, j: (i, 0, j, 0, 0))],
        out_shape=[jax.ShapeDtypeStruct((b, n_pairs, nk // KEY_BLOCK, pw, 2 * KEY_BLOCK), BF16),
                   jax.ShapeDtypeStruct((b, n_pairs, nk // KEY_BLOCK, 2 * KEY_BLOCK, pw), BF16)],
        compiler_params=_params(2),
        name="kv_layout",
    )(k, v)


ATTN_EXIT = 104.0


def _attn_body(x_ref, g_ref, wq_ref, wo_ref, kd_ref, vd_ref, kp_ref, vp_ref, *refs,
               diag_block, blocks_before, report_swept):
    if report_swept:
        o_ref, swept_ref, q_ref, acc_ref, carry_ref = refs
    else:
        o_ref, q_ref, acc_ref, carry_ref = refs
    qb = x_ref.shape[1]
    n_pairs = kd_ref.shape[1]
    pw = 2 * SB_HEAD_DIM
    tile = 2 * KEY_BLOCK
    q_per_kv = wq_ref.shape[1] // (SB_KV_HEADS * SB_HEAD_DIM)
    rows = q_per_kv * qb
    qi = pl.program_id(1)
    d_blk = qi if diag_block is None else diag_block
    last_past = (qi if blocks_before is None else jnp.int32(blocks_before)) - 1

    x = x_ref[0]
    h = _rms(x, g_ref[...]).astype(BF16)
    q = (_dot(h, wq_ref[...]) * (SB_HEAD_DIM ** -0.5)).astype(BF16)
    for pr in range(n_pairs):
        for g in range(q_per_kv):
            c0 = (pr * q_per_kv + g) * pw
            q_ref[pr, g * qb:(g + 1) * qb, :] = q[:, c0:c0 + pw]

    key_i = lax.broadcasted_iota(jnp.int32, (tile, tile), 0)
    key_s = lax.broadcasted_iota(jnp.int32, (tile, tile), 1)
    same_head = (key_i >= KEY_BLOCK) == (key_s >= KEY_BLOCK)
    suffix_ones = ((key_i >= key_s) & same_head).astype(BF16)
    row_q = lax.broadcasted_iota(jnp.int32, (rows, tile), 0) % qb
    col_k = lax.broadcasted_iota(jnp.int32, (rows, tile), 1) % KEY_BLOCK
    causal = col_k < row_q

    def sweep(pr, kt, vb, first):
        z = _dot(q_ref[pr], kt)
        sp = _softplus(z.astype(BF16))
        if first:
            sp = jnp.where(causal, sp, jnp.zeros_like(sp))
        cs = _dot(sp, suffix_ones)
        log_a = z - cs
        if first:
            log_a = jnp.where(causal, log_a, -jnp.inf)
        else:
            log_a = log_a - carry_ref[pr]
        pv = _dot(jnp.exp(log_a).astype(BF16), vb)
        total = totals(cs)
        if first:
            acc_ref[pr] = pv
            carry_ref[pr] = total
        else:
            acc_ref[pr] += pv
            carry_ref[pr] += total

    def totals(cs):
        return jnp.concatenate(
            [jnp.broadcast_to(cs[:, 0:1], (rows, KEY_BLOCK)),
             jnp.broadcast_to(cs[:, KEY_BLOCK:KEY_BLOCK + 1], (rows, KEY_BLOCK))], axis=1)

    def sweep_two(kb):
        combos = [(pr, j) for j in range(2) for pr in range(n_pairs)]
        z = {c: _dot(q_ref[c[0]], kp_ref[0, c[0], kb - c[1]]) for c in combos}
        cs = {c: _dot(_softplus(z[c].astype(BF16)), suffix_ones) for c in combos}
        for pr in range(n_pairs):
            carry = carry_ref[pr]
            newer = z[pr, 0] - cs[pr, 0] - carry
            carry = carry + totals(cs[pr, 0])
            older = z[pr, 1] - cs[pr, 1] - carry
            acc_ref[pr] += (_dot(jnp.exp(newer).astype(BF16), vp_ref[0, pr, kb])
                            + _dot(jnp.exp(older).astype(BF16), vp_ref[0, pr, kb - 1]))
            carry_ref[pr] = carry + totals(cs[pr, 1])

    for pr in range(n_pairs):
        sweep(pr, kd_ref[0, pr, d_blk], vd_ref[0, pr, d_blk], True)

    def more(state):
        kb, swept = state
        return jnp.logical_and(kb >= 0, swept < ATTN_EXIT)

    def step(state):
        kb, _ = state

        def two():
            sweep_two(kb)
            return kb - 2, jnp.min(carry_ref[...])

        def one():
            for pr in range(n_pairs):
                sweep(pr, kp_ref[0, pr, kb], vp_ref[0, pr, kb], False)
            return kb - 1, jnp.min(carry_ref[...])

        return lax.cond(kb >= 1, two, one)

    _, swept = lax.while_loop(more, step, (last_past, jnp.min(carry_ref[...])))
    if report_swept:
        swept_ref[0, 0] = jnp.full(swept_ref.shape[2:], swept, F32)

    o = jnp.concatenate([acc_ref[pr, g * qb:(g + 1) * qb, :]
                         for pr in range(n_pairs) for g in range(q_per_kv)], axis=1)
    o_ref[0] = x + _dot(o.astype(BF16), wo_ref[...])


def _attn(x, g, wq, wo, kd, vd, kp, vp, diag_block, blocks_before, report_swept=False):
    b, l, d = x.shape
    dq = wq.shape[1]
    qb = min(Q_BLOCK, l)
    n_pairs = SB_KV_HEADS // 2
    pw = 2 * SB_HEAD_DIM
    rows = (dq // (SB_KV_HEADS * SB_HEAD_DIM)) * qb

    def whole(a):
        return pl.BlockSpec((1,) + a.shape[1:], lambda i, j: (i, 0, 0, 0, 0))

    out_specs = [pl.BlockSpec((1, qb, d), lambda i, j: (i, j, 0))]
    out_shape = [jax.ShapeDtypeStruct((b, l, d), F32)]
    if report_swept:
        out_specs.append(pl.BlockSpec((1, 1, 8, 128), lambda i, j: (i, j, 0, 0)))
        out_shape.append(jax.ShapeDtypeStruct((b, l // qb, 8, 128), F32))
    return pl.pallas_call(
        functools.partial(_attn_body, diag_block=diag_block, blocks_before=blocks_before,
                          report_swept=report_swept),
        grid=(b, l // qb),
        in_specs=[pl.BlockSpec((1, qb, d), lambda i, j: (i, j, 0)),
                  _resident((1, d)), _resident((d, dq)), _resident((dq, d)),
                  whole(kd), whole(vd), whole(kp), whole(vp)],
        out_specs=out_specs,
        out_shape=out_shape,
        scratch_shapes=[pltpu.VMEM((n_pairs, rows, pw), BF16),
                        pltpu.VMEM((n_pairs, rows, pw), F32),
                        pltpu.VMEM((n_pairs, rows, 2 * KEY_BLOCK), F32)],
        compiler_params=_params(2),
        name="sb_attn",
    )(x, g, wq, wo, kd, vd, kp, vp)


def _pair_major_order(n_heads):
    q_per_kv = n_heads // SB_KV_HEADS
    order = []
    for pr in range(SB_KV_HEADS // 2):
        for g in range(q_per_kv):
            for s in range(2):
                head = (2 * pr + s) * q_per_kv + g
                order.extend(range(head * SB_HEAD_DIM, (head + 1) * SB_HEAD_DIM))
    return jnp.asarray(order, jnp.int32)


def _mamba_body(x_ref, g_ref, wz_ref, wx_ref, wdt_ref, wdtt_ref, cw_ref, cb_ref,
                dtb_ref, dtbt_ref, alog_ref, alogt_ref, dskip_ref, ng_ref, wout_ref,
                s0_ref, c0_ref,
                o_ref, sfin_ref, cfin_ref,
                state_ref, pad_ref, y_ref, xbd_ref, *, chunk, valid, has_init):
    tt = x_ref.shape[1]
    d_inner = wz_ref.shape[1]
    n_heads = wdt_ref.shape[1]
    hpg = n_heads // SSM_GROUPS
    gw = hpg * SSM_HEAD_DIM
    gn = SSM_GROUPS * D_STATE
    t = pl.program_id(1)

    @pl.when(t == 0)
    def _init():
        pad_ref[0:CONV_PAD, :] = jnp.zeros((CONV_PAD, pad_ref.shape[1]), F32)
        xbd_ref[...] = jnp.zeros(xbd_ref.shape, BF16)
        if has_init:
            for g in range(SSM_GROUPS):
                state_ref[g] = s0_ref[0, g].T
            pad_ref[CONV_PAD - (CONV_W - 1):CONV_PAD, :] = c0_ref[0]
        else:
            state_ref[...] = jnp.zeros(state_ref.shape, F32)

    x = x_ref[0]
    h = _rms(x, g_ref[...]).astype(BF16)
    z = _dot(h, wz_ref[...])
    u = _dot(h, wx_ref[...])
    pad_ref[CONV_PAD:CONV_PAD + tt, :] = u
    conv = cb_ref[...] + u * cw_ref[CONV_W - 1:CONV_W, :]
    for k in range(CONV_W - 2, -1, -1):
        off = CONV_PAD - (CONV_W - 1) + k
        conv = conv + pad_ref[off:off + tt, :] * cw_ref[k:k + 1, :]
    xbc = _silu(conv)
    xs = xbc[:, :d_inner]
    bm = xbc[:, d_inner:d_inner + gn].astype(BF16)
    cm = xbc[:, d_inner + gn:].astype(BF16)

    dt = _softplus(_dot(h, wdt_ref[...]) + dtb_ref[...])
    dtt = _softplus(_dot_nt(wdtt_ref[...], h) + dtbt_ref[...])
    if valid < tt:
        dt = jnp.where(lax.broadcasted_iota(jnp.int32, dt.shape, 0) < valid, dt, 0.0)
        dtt = jnp.where(lax.broadcasted_iota(jnp.int32, dtt.shape, 1) < valid, dtt, 0.0)
    a = dt * (-jnp.exp(alog_ref[...]))
    at = dtt * (-jnp.exp(alogt_ref[...]))

    hd = SSM_HEAD_DIM
    pw, qw = 2 * hd, 4 * hd
    half = n_heads // 2
    ci = lax.broadcasted_iota(jnp.int32, (chunk, chunk), 0)
    cj = lax.broadcasted_iota(jnp.int32, (chunk, chunk), 1)
    lower_f = (ci >= cj).astype(F32)
    upper_f = (ci <= cj).astype(F32)
    lane_p = lax.broadcasted_iota(jnp.int32, (chunk, pw), 1)
    first = lane_p < hd
    lower2 = lax.broadcasted_iota(jnp.int32, (chunk, pw), 0) >= lane_p % hd

    def pair_cols(v, p):
        return jnp.where(first, v[:, 2 * p:2 * p + 1], v[:, 2 * p + 1:2 * p + 2])

    xs_bf = xs.astype(BF16)
    quads_per_group = hpg // 4
    for c in range(tt // chunk):
        rows = slice(c * chunk, (c + 1) * chunk)
        acs = _dot_exact(lower_f, a[rows, :])
        acst = _dot_exact(at[:, rows], upper_f)
        acst_p = jnp.concatenate([acst[:half], acst[half:]], axis=1)
        dtt_p = jnp.concatenate([dtt[:half, rows], dtt[half:, rows]], axis=1)
        dw = dt[rows, :] * jnp.exp(acs[chunk - 1:chunk, :] - acs)
        for g in range(SSM_GROUPS):
            b_c = bm[rows, g * D_STATE:(g + 1) * D_STATE]
            c_c = cm[rows, g * D_STATE:(g + 1) * D_STATE]
            cb2 = _dot_nt(c_c, jnp.concatenate([b_c, b_c], axis=0))
            s_g = state_ref[g]
            y_off = _dot(c_c, s_g.astype(BF16))
            xw, dec = [], []
            for q in range(quads_per_group):
                quad = g * quads_per_group + q
                lo = quad * qw
                m_q, e_q = [], []
                for p in (2 * quad, 2 * quad + 1):
                    col = pair_cols(acs, p)
                    seg = col - acst_p[p:p + 1, :]
                    decay = jnp.exp(jnp.where(lower2, seg, -jnp.inf))
                    m_q.append((cb2 * decay * dtt_p[p:p + 1, :]).astype(BF16))
                    e_q.append(jnp.exp(col))
                    dec.append(jnp.exp(col[chunk - 1:chunk, :]))
                    xw.append((xs[rows, p * pw:(p + 1) * pw] * pair_cols(dw, p)).astype(BF16))
                slot = c * (n_heads // 4) + quad
                for k in range(4):
                    xbd_ref[slot, k * chunk:(k + 1) * chunk, k * hd:(k + 1) * hd] = (
                        xs_bf[rows, lo + k * hd:lo + (k + 1) * hd])
                y_q = _dot(jnp.concatenate(m_q, axis=1), xbd_ref[slot])
                y_q = y_q + y_off[:, q * qw:(q + 1) * qw] * jnp.concatenate(e_q, axis=1)
                y_ref[rows, lo:lo + qw] = y_q
            state_ref[g] = (s_g * jnp.concatenate(dec, axis=1)
                            + _dot_tn(b_c, jnp.concatenate(xw, axis=1)))

    y = y_ref[...] + xs * dskip_ref[...]
    y = y * _silu(z)
    normed = []
    for g in range(SSM_GROUPS):
        yg = y[:, g * gw:(g + 1) * gw]
        normed.append(yg * lax.rsqrt(jnp.mean(yg * yg, axis=-1, keepdims=True) + NORM_EPS))
    y = jnp.concatenate(normed, axis=1) * ng_ref[...]
    o_ref[0] = x + _dot(y.astype(BF16), wout_ref[...])

    @pl.when(t == pl.num_programs(1) - 1)
    def _final():
        for g in range(SSM_GROUPS):
            sfin_ref[0, g] = state_ref[g].T
        cfin_ref[0] = pad_ref[CONV_PAD + valid - (CONV_W - 1):CONV_PAD + valid, :]

    pad_ref[0:CONV_PAD, :] = pad_ref[tt:tt + CONV_PAD, :]


def _mamba(x, g, w, s0, c0):
    b, l, d = x.shape
    d_inner = w["wz"].shape[1]
    conv_dim = w["wx"].shape[1]
    n_heads = w["wdt"].shape[1]
    gw = (n_heads // SSM_GROUPS) * SSM_HEAD_DIM
    chunk = SSD_CHUNK
    assert chunk == SSM_HEAD_DIM and (n_heads // SSM_GROUPS) % 4 == 0
    valid = l
    if l < chunk:
        x = jnp.pad(x, ((0, 0), (0, chunk - l), (0, 0)))
    l_pad = x.shape[1]
    tt = min(MAMBA_TILE, l_pad)
    assert l_pad % tt == 0 and tt % chunk == 0
    if l_pad == l:
        valid = tt
    has_init = s0 is not None
    if not has_init:
        s0 = jnp.zeros((1, SSM_GROUPS, gw, D_STATE), F32)
        c0 = jnp.zeros((1, CONV_W - 1, conv_dim), F32)
        init_map = lambda i, j: (0, 0, 0, 0)
        conv_map = lambda i, j: (0, 0, 0)
    else:
        init_map = lambda i, j: (i, 0, 0, 0)
        conv_map = lambda i, j: (i, 0, 0)
    state_spec = pl.BlockSpec((1, SSM_GROUPS, gw, D_STATE), lambda i, j: (i, 0, 0, 0))
    conv_spec = pl.BlockSpec((1, CONV_W - 1, conv_dim), lambda i, j: (i, 0, 0))
    n_slots = (tt // chunk) * (n_heads // 4)
    out, s_fin, c_fin = pl.pallas_call(
        functools.partial(_mamba_body, chunk=chunk, valid=valid, has_init=has_init),
        grid=(b, l_pad // tt),
        in_specs=[pl.BlockSpec((1, tt, d), lambda i, j: (i, j, 0)),
                  _resident((1, d)), _resident((d, d_inner)), _resident((d, conv_dim)),
                  _resident((d, n_heads)), _resident((n_heads, d)),
                  _resident((CONV_W, conv_dim)), _resident((1, conv_dim)),
                  _resident((1, n_heads)), _resident((n_heads, 1)),
                  _resident((1, n_heads)), _resident((n_heads, 1)),
                  _resident((1, d_inner)), _resident((1, d_inner)), _resident((d_inner, d)),
                  pl.BlockSpec((1, SSM_GROUPS, gw, D_STATE), init_map),
                  pl.BlockSpec((1, CONV_W - 1, conv_dim), conv_map)],
        out_specs=[pl.BlockSpec((1, tt, d), lambda i, j: (i, j, 0)), state_spec, conv_spec],
        out_shape=[jax.ShapeDtypeStruct((b, l_pad, d), F32),
                   jax.ShapeDtypeStruct((b, SSM_GROUPS, gw, D_STATE), F32),
                   jax.ShapeDtypeStruct((b, CONV_W - 1, conv_dim), F32)],
        scratch_shapes=[pltpu.VMEM((SSM_GROUPS, D_STATE, gw), F32),
                        pltpu.VMEM((CONV_PAD + tt, conv_dim), F32),
                        pltpu.VMEM((tt, d_inner), F32),
                        pltpu.VMEM((n_slots, 4 * chunk, 4 * SSM_HEAD_DIM), BF16)],
        compiler_params=_params(2),
        name="mamba2",
    )(x, g, w["wz"], w["wx"], w["wdt"], w["wdtt"], w["conv_w"], w["conv_b"],
      w["dt_bias"], w["dt_bias_t"], w["a_log"], w["a_log_t"], w["d_skip"], w["norm_g"],
      w["wout"], s0, c0)
    return out[:, :l], s_fin, c_fin


def _prepare_mamba(w_in, conv_w, conv_b, dt_bias, a_log, d_skip, norm_g, w_out):
    d_inner = w_out.shape[0]
    n_heads = dt_bias.shape[0]
    conv_dim = conv_w.shape[1]
    wdt = w_in[:, d_inner + conv_dim:].astype(BF16)
    eo = jnp.concatenate([jnp.arange(0, n_heads, 2), jnp.arange(1, n_heads, 2)])
    return {
        "wz": w_in[:, :d_inner].astype(BF16),
        "wx": w_in[:, d_inner:d_inner + conv_dim].astype(BF16),
        "wdt": wdt,
        "wdtt": wdt.T[eo],
        "conv_w": conv_w,
        "conv_b": conv_b.reshape(1, conv_dim),
        "dt_bias": dt_bias.reshape(1, n_heads),
        "dt_bias_t": dt_bias[eo].reshape(n_heads, 1),
        "a_log": a_log.reshape(1, n_heads),
        "a_log_t": a_log[eo].reshape(n_heads, 1),
        "d_skip": jnp.repeat(d_skip, SSM_HEAD_DIM).reshape(1, d_inner),
        "norm_g": norm_g.reshape(1, d_inner),
        "wout": w_out.astype(BF16),
    }


NEAR_PAST_BLOCKS = 4


def _attn_with_past(layer, k_past, v_past):
    b, past = k_past.shape[:2]

    def sweep(n_keys, report):
        kp, vp = _kv_layout(k_past[:, past - n_keys:].reshape(b, n_keys, -1),
                            v_past[:, past - n_keys:].reshape(b, n_keys, -1))
        return _attn(*layer, kp, vp, 0, n_keys // KEY_BLOCK, report_swept=report)

    near = min(NEAR_PAST_BLOCKS * KEY_BLOCK, past)
    if near == past:
        return sweep(past, False)[0]
    out_near, swept = sweep(near, True)
    return lax.cond(jnp.min(swept) >= ATTN_EXIT, lambda: out_near, lambda: sweep(past, False)[0])


def _trunk(x, p, ssm0, conv0, k_past, v_past, wts):
    b, l, d = x.shape
    t = b * l
    depth = wts["ffn_norm"].shape[0]
    n_a = len(wts["mamba"])
    assert 1 <= n_a < depth
    ssm_out, conv_out = [], []
    k_new = v_new = kd = vd = None
    xf = x.reshape(t, d)
    for i in range(depth):
        xf, = _token_stage(xf, wts["ffn"][i][0])
        if i < n_a:
            s0 = None if ssm0 is None else ssm0[i].reshape(b, SSM_GROUPS, -1, D_STATE)
            c0 = None if conv0 is None else conv0[i]
            x3, s_fin, c_fin = _mamba(xf.reshape(b, l, d), wts["mix_norm"][i], wts["mamba"][i], s0, c0)
            xf = x3.reshape(t, d)
            ssm_out.append(s_fin.reshape(b, -1, SSM_HEAD_DIM, D_STATE))
            conv_out.append(c_fin)
        else:
            jb = i - n_a
            layer = (xf.reshape(b, l, d), wts["mix_norm"][i], wts["wq"][jb], wts["wo"][jb], kd, vd)
            if k_past is None:
                x3, = _attn(*layer, kd, vd, None, None)
            else:
                x3 = _attn_with_past(layer, k_past, v_past)
            xf = x3.reshape(t, d)
        outs = _token_stage(xf, wts["ffn"][i][1], ple=wts["ple"][i], p=p[i].reshape(t, -1),
                            final_g=wts["final_norm"] if i == depth - 1 else None,
                            kv=wts["kv"] if i + 1 == n_a else None)
        xf = outs[0]
        if i + 1 == n_a:
            k_new, v_new = outs[1], outs[2]
            k3 = k_new.reshape(b, l, -1)
            v3 = v_new.reshape(b, l, -1)
            if k_past is None:
                kd, vd = _kv_layout(k3, v3)
            else:
                assert k_past.shape[1] % KEY_BLOCK == 0 and l <= KEY_BLOCK
                fill = ((0, 0), (0, KEY_BLOCK - l), (0, 0))
                kd, vd = _kv_layout(jnp.pad(k3, fill), jnp.pad(v3, fill))
    kv_shape = (b, l, SB_KV_HEADS, SB_HEAD_DIM)
    return (xf.reshape(b, l, d), jnp.stack(ssm_out), jnp.stack(conv_out),
            k_new.reshape(kv_shape), v_new.reshape(kv_shape))


def kernel(x_prompt, x_sample, p_prompt, p_sample, state_ssm, state_conv, cache_k, cache_v,
           ffn_norm, ffn_w_gate, ffn_w_up, ffn_w_down, mix_norm,
           ssm_w_in, ssm_conv_w, ssm_conv_b, ssm_dt_bias, ssm_a_log, ssm_d, ssm_norm, ssm_w_out,
           kv_norm, w_k, w_v, sb_w_q, sb_w_o,
           ple_norm, ple_w_gate, ple_w_proj, final_norm):
    depth, d = mix_norm.shape
    n_a = ssm_w_in.shape[0]
    n_b = sb_w_q.shape[0]
    head_order = _pair_major_order(sb_w_q.shape[2] // SB_HEAD_DIM)
    wts = {
        "ffn_norm": ffn_norm,
        "ffn": [[{"g": ffn_norm[i, s].reshape(1, d),
                  "wg": ffn_w_gate[i, s].astype(BF16),
                  "wu": ffn_w_up[i, s].astype(BF16),
                  "wd": ffn_w_down[i, s].astype(BF16)} for s in range(2)] for i in range(depth)],
        "mix_norm": [mix_norm[i].reshape(1, d) for i in range(depth)],
        "mamba": [_prepare_mamba(ssm_w_in[i], ssm_conv_w[i], ssm_conv_b[i], ssm_dt_bias[i],
                                 ssm_a_log[i], ssm_d[i], ssm_norm[i], ssm_w_out[i])
                  for i in range(n_a)],
        "kv": {"g": kv_norm.reshape(1, d), "wk": w_k.astype(BF16), "wv": w_v.astype(BF16)},
        "wq": [sb_w_q[j][:, head_order].astype(BF16) for j in range(n_b)],
        "wo": [sb_w_o[j][head_order, :].astype(BF16) for j in range(n_b)],
        "ple": [{"g": ple_norm[i].reshape(1, d),
                 "wgate": ple_w_gate[i].astype(BF16),
                 "wproj": ple_w_proj[i].astype(BF16)} for i in range(depth)],
        "final_norm": final_norm.reshape(1, d),
    }
    y_p, ssm_p, conv_p, k_p, v_p = _trunk(x_prompt, p_prompt, None, None, None, None, wts)
    y_s, ssm_s, conv_s, k_s, v_s = _trunk(x_sample, p_sample, state_ssm, state_conv,
                                          cache_k, cache_v, wts)
    return (y_p, y_s, ssm_p, conv_p, k_p, v_p, ssm_s, conv_s, k_s, v_s)
```

```python
import functools

import jax
import jax.numpy as jnp
from jax import lax
from jax.experimental import pallas as pl
from jax.experimental.pallas import tpu as pltpu

F32 = jnp.float32
BF16 = jnp.bfloat16
NORM_EPS = 1e-6

SSM_HEAD_DIM = 64
SSM_GROUPS = 4
D_STATE = 128
CONV_W = 4
SSD_CHUNK = 64
SB_HEAD_DIM = 64
SB_KV_HEADS = 4
Q_BLOCK = 128
KEY_BLOCK = 128

VMEM_LIMIT_BYTES = 56 * 1024 * 1024
TOKEN_TILE = 512
MAMBA_TILE = 512
MAMBA_SEGMENT = 512
MXU_TILE = 256
CONV_PAD = 8


def _rms(x, g):
    return x * lax.rsqrt(jnp.mean(x * x, axis=-1, keepdims=True) + NORM_EPS) * g


def _sigmoid(v):
    return 1.0 / (1.0 + jnp.exp(-v))


def _silu(v):
    return v * _sigmoid(v)


def _softplus(v):
    return jnp.maximum(v, 0.0) + jnp.log(1.0 + jnp.exp(-jnp.abs(v)))


def _dot(a, b):
    return jnp.dot(a, b, preferred_element_type=F32)


def _dot_nt(a, b):
    return lax.dot_general(a, b, (((1,), (1,)), ((), ())), preferred_element_type=F32)


def _dot_tn(a, b):
    return lax.dot_general(a, b, (((0,), (0,)), ((), ())), preferred_element_type=F32)


def _dot_exact(a, b):
    return jnp.dot(a, b, precision=lax.Precision.HIGHEST, preferred_element_type=F32)


def _resident(shape):
    zeros = (0,) * len(shape)
    return pl.BlockSpec(shape, lambda *_: zeros, pipeline_mode=pl.Buffered(1))


def _params(n_axes):
    return pltpu.CompilerParams(dimension_semantics=("arbitrary",) * n_axes,
                                vmem_limit_bytes=VMEM_LIMIT_BYTES)


def _ffn_chunks(d_ff):
    if d_ff % MXU_TILE:
        return [(0, d_ff)]
    tiles = d_ff // MXU_TILE
    first = (tiles + 1) // 2 * MXU_TILE
    return [(0, first), (first, d_ff)] if first < d_ff else [(0, d_ff)]


def _token_body(*refs, with_ple, final, with_kv):
    it = iter(refs)
    x_ref, g_ref, wg_ref, wu_ref, wd_ref = (next(it) for _ in range(5))
    if with_ple:
        p_ref, pg_ref, wgate_ref, wproj_ref = (next(it) for _ in range(4))
    if final:
        fin_ref = next(it)
    if with_kv:
        kvg_ref, wk_ref, wv_ref = (next(it) for _ in range(3))
    o_ref = next(it)

    x = x_ref[...]
    h = _rms(x, g_ref[...]).astype(BF16)
    acc = jnp.zeros_like(x)
    for lo, hi in _ffn_chunks(wg_ref.shape[1]):
        gate = _dot(h, wg_ref[:, lo:hi])
        up = _dot(h, wu_ref[:, lo:hi])
        act = (_silu(gate) * up).astype(BF16)
        acc = acc + _dot(act, wd_ref[lo:hi, :])
    y = x + 0.5 * acc
    if with_ple:
        hp = _rms(y, pg_ref[...]).astype(BF16)
        y = y + _sigmoid(_dot(hp, wgate_ref[...])) * _dot(p_ref[...].astype(BF16), wproj_ref[...])
    if final:
        y = _rms(y, fin_ref[...])
    o_ref[...] = y
    if with_kv:
        k_ref, v_ref = next(it), next(it)
        hk = _rms(y, kvg_ref[...]).astype(BF16)
        k_ref[...] = _dot(hk, wk_ref[...])
        v_ref[...] = _dot(hk, wv_ref[...])


def _token_stage(x, ffn, ple=None, p=None, final_g=None, kv=None):
    t, d = x.shape
    tm = min(TOKEN_TILE, t)
    row = lambda n: pl.BlockSpec((tm, n), lambda i: (i, 0))
    args = [x, ffn["g"], ffn["wg"], ffn["wu"], ffn["wd"]]
    in_specs = [row(d)] + [_resident(a.shape) for a in args[1:]]
    if ple is not None:
        extra = [ple["g"], ple["wgate"], ple["wproj"]]
        args += [p] + extra
        in_specs += [row(p.shape[1])] + [_resident(a.shape) for a in extra]
    if final_g is not None:
        args.append(final_g)
        in_specs.append(_resident(final_g.shape))
    out_specs = [row(d)]
    out_shape = [jax.ShapeDtypeStruct((t, d), F32)]
    if kv is not None:
        extra = [kv["g"], kv["wk"], kv["wv"]]
        args += extra
        in_specs += [_resident(a.shape) for a in extra]
        dkv = kv["wk"].shape[1]
        out_specs += [row(dkv), row(dkv)]
        out_shape += [jax.ShapeDtypeStruct((t, dkv), F32)] * 2
    return pl.pallas_call(
        functools.partial(_token_body, with_ple=ple is not None, final=final_g is not None,
                          with_kv=kv is not None),
        grid=(t // tm,),
        in_specs=in_specs,
        out_specs=out_specs,
        out_shape=out_shape,
        compiler_params=_params(1),
        name="token_stage",
    )(*args)


def _kv_layout_body(k_ref, v_ref, kt_ref, vb_ref):
    tm = k_ref.shape[1]
    hd = SB_HEAD_DIM
    pw = 2 * hd
    kt = k_ref[0].T.astype(BF16)
    v = v_ref[0].astype(BF16)
    first_head = lax.broadcasted_iota(jnp.int32, (KEY_BLOCK, pw), 1) < hd
    zeros_k = jnp.zeros((hd, KEY_BLOCK), BF16)
    zeros_v = jnp.zeros((KEY_BLOCK, pw), BF16)
    for pr in range(SB_KV_HEADS // 2):
        for kb in range(tm // KEY_BLOCK):
            keys = slice(kb * KEY_BLOCK, (kb + 1) * KEY_BLOCK)
            kt_ref[0, pr, kb, 0:hd, 0:KEY_BLOCK] = kt[(2 * pr) * hd:(2 * pr + 1) * hd, keys]
            kt_ref[0, pr, kb, 0:hd, KEY_BLOCK:2 * KEY_BLOCK] = zeros_k
            kt_ref[0, pr, kb, hd:pw, 0:KEY_BLOCK] = zeros_k
            kt_ref[0, pr, kb, hd:pw, KEY_BLOCK:2 * KEY_BLOCK] = kt[(2 * pr + 1) * hd:(2 * pr + 2) * hd, keys]
            vp = v[keys, pr * pw:(pr + 1) * pw]
            vb_ref[0, pr, kb, 0:KEY_BLOCK, :] = jnp.where(first_head, vp, zeros_v)
            vb_ref[0, pr, kb, KEY_BLOCK:2 * KEY_BLOCK, :] = jnp.where(first_head, zeros_v, vp)


def _kv_layout(k, v):
    b, nk, dkv = k.shape
    tm = min(TOKEN_TILE, nk)
    n_pairs = SB_KV_HEADS // 2
    pw = 2 * SB_HEAD_DIM
    nb = tm // KEY_BLOCK
    return pl.pallas_call(
        _kv_layout_body,
        grid=(b, nk // tm),
        in_specs=[pl.BlockSpec((1, tm, dkv), lambda i, j: (i, j, 0)),
                  pl.BlockSpec((1, tm, dkv), lambda i, j: (i, j, 0))],
        out_specs=[pl.BlockSpec((1, n_pairs, nb, pw, 2 * KEY_BLOCK), lambda i, j: (i, 0, j, 0, 0)),
                   pl.BlockSpec((1, n_pairs, nb, 2 * KEY_BLOCK, pw), lambda i, j: (i, 0, j, 0, 0))],
        out_shape=[jax.ShapeDtypeStruct((b, n_pairs, nk // KEY_BLOCK, pw, 2 * KEY_BLOCK), BF16),
                   jax.ShapeDtypeStruct((b, n_pairs, nk // KEY_BLOCK, 2 * KEY_BLOCK, pw), BF16)],
        compiler_params=_params(2),
        name="kv_layout",
    )(k, v)


ATTN_EXIT = 104.0


def _attn_body(x_ref, g_ref, wq_ref, wo_ref, kd_ref, vd_ref, kp_ref, vp_ref, *refs,
               diag_block, blocks_before, report_swept):
    if report_swept:
        o_ref, swept_ref, q_ref, acc_ref, carry_ref = refs
    else:
        o_ref, q_ref, acc_ref, carry_ref = refs
    qb = x_ref.shape[1]
    n_pairs = kd_ref.shape[1]
    pw = 2 * SB_HEAD_DIM
    tile = 2 * KEY_BLOCK
    q_per_kv = wq_ref.shape[1] // (SB_KV_HEADS * SB_HEAD_DIM)
    rows = q_per_kv * qb
    qi = pl.program_id(1)
    d_blk = qi if diag_block is None else diag_block
    last_past = (qi if blocks_before is None else jnp.int32(blocks_before)) - 1

    x = x_ref[0]
    h = _rms(x, g_ref[...]).astype(BF16)
    q = (_dot(h, wq_ref[...]) * (SB_HEAD_DIM ** -0.5)).astype(BF16)
    for pr in range(n_pairs):
        for g in range(q_per_kv):
            c0 = (pr * q_per_kv + g) * pw
            q_ref[pr, g * qb:(g + 1) * qb, :] = q[:, c0:c0 + pw]

    key_i = lax.broadcasted_iota(jnp.int32, (tile, tile), 0)
    key_s = lax.broadcasted_iota(jnp.int32, (tile, tile), 1)
    same_head = (key_i >= KEY_BLOCK) == (key_s >= KEY_BLOCK)
    suffix_ones = ((key_i >= key_s) & same_head).astype(BF16)
    row_q = lax.broadcasted_iota(jnp.int32, (rows, tile), 0) % qb
    col_k = lax.broadcasted_iota(jnp.int32, (rows, tile), 1) % KEY_BLOCK
    causal = col_k < row_q

    def sweep(pr, kt, vb, first):
        z = _dot(q_ref[pr], kt)
        sp = _softplus(z.astype(BF16))
        if first:
            sp = jnp.where(causal, sp, jnp.zeros_like(sp))
        cs = _dot(sp, suffix_ones)
        log_a = z - cs
        if first:
            log_a = jnp.where(causal, log_a, -jnp.inf)
        else:
            log_a = log_a - carry_ref[pr]
        pv = _dot(jnp.exp(log_a).astype(BF16), vb)
        total = totals(cs)
        if first:
            acc_ref[pr] = pv
            carry_ref[pr] = total
        else:
            acc_ref[pr] += pv
            carry_ref[pr] += total

    def totals(cs):
        return jnp.concatenate(
            [jnp.broadcast_to(cs[:, 0:1], (rows, KEY_BLOCK)),
             jnp.broadcast_to(cs[:, KEY_BLOCK:KEY_BLOCK + 1], (rows, KEY_BLOCK))], axis=1)

    def sweep_two(kb):
        combos = [(pr, j) for j in range(2) for pr in range(n_pairs)]
        z = {c: _dot(q_ref[c[0]], kp_ref[0, c[0], kb - c[1]]) for c in combos}
        cs = {c: _dot(_softplus(z[c].astype(BF16)), suffix_ones) for c in combos}
        for pr in range(n_pairs):
            carry = carry_ref[pr]
            newer = z[pr, 0] - cs[pr, 0] - carry
            carry = carry + totals(cs[pr, 0])
            older = z[pr, 1] - cs[pr, 1] - carry
            acc_ref[pr] += (_dot(jnp.exp(newer).astype(BF16), vp_ref[0, pr, kb])
                            + _dot(jnp.exp(older).astype(BF16), vp_ref[0, pr, kb - 1]))
            carry_ref[pr] = carry + totals(cs[pr, 1])

    for pr in range(n_pairs):
        sweep(pr, kd_ref[0, pr, d_blk], vd_ref[0, pr, d_blk], True)

    def more(state):
        kb, swept = state
        return jnp.logical_and(kb >= 0, swept < ATTN_EXIT)

    def step(state):
        kb, _ = state

        def two():
            sweep_two(kb)
            return kb - 2, jnp.min(carry_ref[...])

        def one():
            for pr in range(n_pairs):
                sweep(pr, kp_ref[0, pr, kb], vp_ref[0, pr, kb], False)
            return kb - 1, jnp.min(carry_ref[...])

        return lax.cond(kb >= 1, two, one)

    _, swept = lax.while_loop(more, step, (last_past, jnp.min(carry_ref[...])))
    if report_swept:
        swept_ref[0, 0] = jnp.full(swept_ref.shape[2:], swept, F32)

    o = jnp.concatenate([acc_ref[pr, g * qb:(g + 1) * qb, :]
                         for pr in range(n_pairs) for g in range(q_per_kv)], axis=1)
    o_ref[0] = x + _dot(o.astype(BF16), wo_ref[...])


def _attn(x, g, wq, wo, kd, vd, kp, vp, diag_block, blocks_before, report_swept=False):
    b, l, d = x.shape
    dq = wq.shape[1]
    qb = min(Q_BLOCK, l)
    n_pairs = SB_KV_HEADS // 2
    pw = 2 * SB_HEAD_DIM
    rows = (dq // (SB_KV_HEADS * SB_HEAD_DIM)) * qb

    def whole(a):
        return pl.BlockSpec((1,) + a.shape[1:], lambda i, j: (i, 0, 0, 0, 0))

    out_specs = [pl.BlockSpec((1, qb, d), lambda i, j: (i, j, 0))]
    out_shape = [jax.ShapeDtypeStruct((b, l, d), F32)]
    if report_swept:
        out_specs.append(pl.BlockSpec((1, 1, 8, 128), lambda i, j: (i, j, 0, 0)))
        out_shape.append(jax.ShapeDtypeStruct((b, l // qb, 8, 128), F32))
    return pl.pallas_call(
        functools.partial(_attn_body, diag_block=diag_block, blocks_before=blocks_before,
                          report_swept=report_swept),
        grid=(b, l // qb),
        in_specs=[pl.BlockSpec((1, qb, d), lambda i, j: (i, j, 0)),
                  _resident((1, d)), _resident((d, dq)), _resident((dq, d)),
                  whole(kd), whole(vd), whole(kp), whole(vp)],
        out_specs=out_specs,
        out_shape=out_shape,
        scratch_shapes=[pltpu.VMEM((n_pairs, rows, pw), BF16),
                        pltpu.VMEM((n_pairs, rows, pw), F32),
                        pltpu.VMEM((n_pairs, rows, 2 * KEY_BLOCK), F32)],
        compiler_params=_params(2),
        name="sb_attn",
    )(x, g, wq, wo, kd, vd, kp, vp)


def _pair_major_order(n_heads):
    q_per_kv = n_heads // SB_KV_HEADS
    order = []
    for pr in range(SB_KV_HEADS // 2):
        for g in range(q_per_kv):
            for s in range(2):
                head = (2 * pr + s) * q_per_kv + g
                order.extend(range(head * SB_HEAD_DIM, (head + 1) * SB_HEAD_DIM))
    return jnp.asarray(order, jnp.int32)


def _mamba_body(x_ref, g_ref, wz_ref, wx_ref, wdt_ref, wdtt_ref, cw_ref, cb_ref,
                dtb_ref, dtbt_ref, alog_ref, alogt_ref, dskip_ref, ng_ref, wout_ref,
                s0_ref, c0_ref,
                o_ref, sfin_ref, cfin_ref,
                state_ref, pad_ref, y_ref, xbd_ref, xs_ref, bc_ref, *, chunk, valid, has_init):
    tt = x_ref.shape[1]
    d_inner = wz_ref.shape[1]
    n_heads = wdt_ref.shape[1]
    hpg = n_heads // SSM_GROUPS
    gw = hpg * SSM_HEAD_DIM
    gn = SSM_GROUPS * D_STATE
    t = pl.program_id(1)

    @pl.when(t == 0)
    def _init():
        pad_ref[0:CONV_PAD, :] = jnp.zeros((CONV_PAD, pad_ref.shape[1]), F32)
        xbd_ref[...] = jnp.zeros(xbd_ref.shape, BF16)
        if has_init:
            for g in range(SSM_GROUPS):
                state_ref[g] = s0_ref[0, g].T
            pad_ref[CONV_PAD - (CONV_W - 1):CONV_PAD, :] = c0_ref[0]
        else:
            state_ref[...] = jnp.zeros(state_ref.shape, F32)

    x = x_ref[0]
    h = _rms(x, g_ref[...]).astype(BF16)
    for lo in range(0, wx_ref.shape[1], MAMBA_SEGMENT):
        cols = slice(lo, lo + MAMBA_SEGMENT)
        u = _dot(h, wx_ref[:, cols])
        pad_ref[CONV_PAD:CONV_PAD + tt, cols] = u
        conv = cb_ref[:, cols] + u * cw_ref[CONV_W - 1:CONV_W, cols]
        for k in range(CONV_W - 2, -1, -1):
            off = CONV_PAD - (CONV_W - 1) + k
            conv = conv + pad_ref[off:off + tt, cols] * cw_ref[k:k + 1, cols]
        act = _silu(conv)
        if lo < d_inner:
            xs_ref[:, cols] = act
        else:
            bc_ref[:, lo - d_inner:lo - d_inner + MAMBA_SEGMENT] = act.astype(BF16)
    xs = xs_ref[...]
    bm = bc_ref[:, :gn]
    cm = bc_ref[:, gn:]

    dt = _softplus(_dot(h, wdt_ref[...]) + dtb_ref[...])
    dtt = _softplus(_dot_nt(wdtt_ref[...], h) + dtbt_ref[...])
    if valid < tt:
        dt = jnp.where(lax.broadcasted_iota(jnp.int32, dt.shape, 0) < valid, dt, 0.0)
        dtt = jnp.where(lax.broadcasted_iota(jnp.int32, dtt.shape, 1) < valid, dtt, 0.0)
    a = dt * (-jnp.exp(alog_ref[...]))
    at = dtt * (-jnp.exp(alogt_ref[...]))

    hd = SSM_HEAD_DIM
    pw, qw = 2 * hd, 4 * hd
    half = n_heads // 2
    ci = lax.broadcasted_iota(jnp.int32, (chunk, chunk), 0)
    cj = lax.broadcasted_iota(jnp.int32, (chunk, chunk), 1)
    lower_f = (ci >= cj).astype(F32)
    upper_f = (ci <= cj).astype(F32)
    lane_p = lax.broadcasted_iota(jnp.int32, (chunk, pw), 1)
    first = lane_p < hd
    lower2 = lax.broadcasted_iota(jnp.int32, (chunk, pw), 0) >= lane_p % hd

    def pair_cols(v, p):
        return jnp.where(first, v[:, 2 * p:2 * p + 1], v[:, 2 * p + 1:2 * p + 2])

    xs_bf = xs.astype(BF16)
    quads_per_group = hpg // 4
    for c in range(tt // chunk):
        rows = slice(c * chunk, (c + 1) * chunk)
        acs = _dot_exact(lower_f, a[rows, :])
        acst = _dot_exact(at[:, rows], upper_f)
        acst_p = jnp.concatenate([acst[:half], acst[half:]], axis=1)
        dtt_p = jnp.concatenate([dtt[:half, rows], dtt[half:, rows]], axis=1)
        dw = dt[rows, :] * jnp.exp(acs[chunk - 1:chunk, :] - acs)
        for g in range(SSM_GROUPS):
            b_c = bm[rows, g * D_STATE:(g + 1) * D_STATE]
            c_c = cm[rows, g * D_STATE:(g + 1) * D_STATE]
            cb2 = _dot_nt(c_c, jnp.concatenate([b_c, b_c], axis=0))
            s_g = state_ref[g]
            y_off = _dot(c_c, s_g.astype(BF16))
            xw, dec = [], []
            for q in range(quads_per_group):
                quad = g * quads_per_group + q
                lo = quad * qw
                m_q, e_q = [], []
                for p in (2 * quad, 2 * quad + 1):
                    col = pair_cols(acs, p)
                    seg = col - acst_p[p:p + 1, :]
                    decay = jnp.exp(jnp.where(lower2, seg, -jnp.inf))
                    m_q.append((cb2 * decay * dtt_p[p:p + 1, :]).astype(BF16))
                    e_q.append(jnp.exp(col))
                    dec.append(jnp.exp(col[chunk - 1:chunk, :]))
                    xw.append((xs[rows, p * pw:(p + 1) * pw] * pair_cols(dw, p)).astype(BF16))
                slot = c * (n_heads // 4) + quad
                for k in range(4):
                    xbd_ref[slot, k * chunk:(k + 1) * chunk, k * hd:(k + 1) * hd] = (
                        xs_bf[rows, lo + k * hd:lo + (k + 1) * hd])
                y_q = _dot(jnp.concatenate(m_q, axis=1), xbd_ref[slot])
                y_q = y_q + y_off[:, q * qw:(q + 1) * qw] * jnp.concatenate(e_q, axis=1)
                y_ref[rows, lo:lo + qw] = y_q
            state_ref[g] = (s_g * jnp.concatenate(dec, axis=1)
                            + _dot_tn(b_c, jnp.concatenate(xw, axis=1)))

    y = y_ref[...] + xs * dskip_ref[...]
    y = y * _silu(_dot(h, wz_ref[...]))
    normed = []
    for g in range(SSM_GROUPS):
        yg = y[:, g * gw:(g + 1) * gw]
        normed.append(yg * lax.rsqrt(jnp.mean(yg * yg, axis=-1, keepdims=True) + NORM_EPS))
    y = jnp.concatenate(normed, axis=1) * ng_ref[...]
    o_ref[0] = x + _dot(y.astype(BF16), wout_ref[...])

    @pl.when(t == pl.num_programs(1) - 1)
    def _final():
        for g in range(SSM_GROUPS):
            sfin_ref[0, g] = state_ref[g].T
        cfin_ref[0] = pad_ref[CONV_PAD + valid - (CONV_W - 1):CONV_PAD + valid, :]

    pad_ref[0:CONV_PAD, :] = pad_ref[tt:tt + CONV_PAD, :]


def _mamba(x, g, w, s0, c0):
    b, l, d = x.shape
    d_inner = w["wz"].shape[1]
    conv_dim = w["wx"].shape[1]
    n_heads = w["wdt"].shape[1]
    gw = (n_heads // SSM_GROUPS) * SSM_HEAD_DIM
    chunk = SSD_CHUNK
    assert chunk == SSM_HEAD_DIM and (n_heads // SSM_GROUPS) % 4 == 0
    valid = l
    if l < chunk:
        x = jnp.pad(x, ((0, 0), (0, chunk - l), (0, 0)))
    l_pad = x.shape[1]
    tt = min(MAMBA_TILE, l_pad)
    assert l_pad % tt == 0 and tt % chunk == 0
    if l_pad == l:
        valid = tt
    has_init = s0 is not None
    if not has_init:
        s0 = jnp.zeros((1, SSM_GROUPS, gw, D_STATE), F32)
        c0 = jnp.zeros((1, CONV_W - 1, conv_dim), F32)
        init_map = lambda i, j: (0, 0, 0, 0)
        conv_map = lambda i, j: (0, 0, 0)
    else:
        init_map = lambda i, j: (i, 0, 0, 0)
        conv_map = lambda i, j: (i, 0, 0)
    state_spec = pl.BlockSpec((1, SSM_GROUPS, gw, D_STATE), lambda i, j: (i, 0, 0, 0))
    conv_spec = pl.BlockSpec((1, CONV_W - 1, conv_dim), lambda i, j: (i, 0, 0))
    n_slots = (tt // chunk) * (n_heads // 4)
    out, s_fin, c_fin = pl.pallas_call(
        functools.partial(_mamba_body, chunk=chunk, valid=valid, has_init=has_init),
        grid=(b, l_pad // tt),
        in_specs=[pl.BlockSpec((1, tt, d), lambda i, j: (i, j, 0)),
                  _resident((1, d)), _resident((d, d_inner)), _resident((d, conv_dim)),
                  _resident((d, n_heads)), _resident((n_heads, d)),
                  _resident((CONV_W, conv_dim)), _resident((1, conv_dim)),
                  _resident((1, n_heads)), _resident((n_heads, 1)),
                  _resident((1, n_heads)), _resident((n_heads, 1)),
                  _resident((1, d_inner)), _resident((1, d_inner)), _resident((d_inner, d)),
                  pl.BlockSpec((1, SSM_GROUPS, gw, D_STATE), init_map),
                  pl.BlockSpec((1, CONV_W - 1, conv_dim), conv_map)],
        out_specs=[pl.BlockSpec((1, tt, d), lambda i, j: (i, j, 0)), state_spec, conv_spec],
        out_shape=[jax.ShapeDtypeStruct((b, l_pad, d), F32),
                   jax.ShapeDtypeStruct((b, SSM_GROUPS, gw, D_STATE), F32),
                   jax.ShapeDtypeStruct((b, CONV_W - 1, conv_dim), F32)],
        scratch_shapes=[pltpu.VMEM((SSM_GROUPS, D_STATE, gw), F32),
                        pltpu.VMEM((CONV_PAD + tt, conv_dim), F32),
                        pltpu.VMEM((tt, d_inner), F32),
                        pltpu.VMEM((n_slots, 4 * chunk, 4 * SSM_HEAD_DIM), BF16),
                        pltpu.VMEM((tt, d_inner), F32),
                        pltpu.VMEM((tt, conv_dim - d_inner), BF16)],
        compiler_params=_params(2),
        name="mamba2",
    )(x, g, w["wz"], w["wx"], w["wdt"], w["wdtt"], w["conv_w"], w["conv_b"],
      w["dt_bias"], w["dt_bias_t"], w["a_log"], w["a_log_t"], w["d_skip"], w["norm_g"],
      w["wout"], s0, c0)
    return out[:, :l], s_fin, c_fin


def _prepare_mamba(w_in, conv_w, conv_b, dt_bias, a_log, d_skip, norm_g, w_out):
    d_inner = w_out.shape[0]
    n_heads = dt_bias.shape[0]
    conv_dim = conv_w.shape[1]
    wdt = w_in[:, d_inner + conv_dim:].astype(BF16)
    eo = jnp.concatenate([jnp.arange(0, n_heads, 2), jnp.arange(1, n_heads, 2)])
    return {
        "wz": w_in[:, :d_inner].astype(BF16),
        "wx": w_in[:, d_inner:d_inner + conv_dim].astype(BF16),
        "wdt": wdt,
        "wdtt": wdt.T[eo],
        "conv_w": conv_w,
        "conv_b": conv_b.reshape(1, conv_dim),
        "dt_bias": dt_bias.reshape(1, n_heads),
        "dt_bias_t": dt_bias[eo].reshape(n_heads, 1),
        "a_log": a_log.reshape(1, n_heads),
        "a_log_t": a_log[eo].reshape(n_heads, 1),
        "d_skip": jnp.repeat(d_skip, SSM_HEAD_DIM).reshape(1, d_inner),
        "norm_g": norm_g.reshape(1, d_inner),
        "wout": w_out.astype(BF16),
    }


NEAR_PAST_BLOCKS = 4


def _attn_with_past(layer, k_past, v_past):
    b, past = k_past.shape[:2]

    def sweep(n_keys, report):
        kp, vp = _kv_layout(k_past[:, past - n_keys:].reshape(b, n_keys, -1),
                            v_past[:, past - n_keys:].reshape(b, n_keys, -1))
        return _attn(*layer, kp, vp, 0, n_keys // KEY_BLOCK, report_swept=report)

    near = min(NEAR_PAST_BLOCKS * KEY_BLOCK, past)
    if near == past:
        return sweep(past, False)[0]
    out_near, swept = sweep(near, True)
    return lax.cond(jnp.min(swept) >= ATTN_EXIT, lambda: out_near, lambda: sweep(past, False)[0])


def _trunk(x, p, ssm0, conv0, k_past, v_past, wts):
    b, l, d = x.shape
    t = b * l
    depth = wts["ffn_norm"].shape[0]
    n_a = len(wts["mamba"])
    assert 1 <= n_a < depth
    ssm_out, conv_out = [], []
    k_new = v_new = kd = vd = None
    xf = x.reshape(t, d)
    for i in range(depth):
        xf, = _token_stage(xf, wts["ffn"][i][0])
        if i < n_a:
            s0 = None if ssm0 is None else ssm0[i].reshape(b, SSM_GROUPS, -1, D_STATE)
            c0 = None if conv0 is None else conv0[i]
            x3, s_fin, c_fin = _mamba(xf.reshape(b, l, d), wts["mix_norm"][i], wts["mamba"][i], s0, c0)
            xf = x3.reshape(t, d)
            ssm_out.append(s_fin.reshape(b, -1, SSM_HEAD_DIM, D_STATE))
            conv_out.append(c_fin)
        else:
            jb = i - n_a
            layer = (xf.reshape(b, l, d), wts["mix_norm"][i], wts["wq"][jb], wts["wo"][jb], kd, vd)
            if k_past is None:
                x3, = _attn(*layer, kd, vd, None, None)
            else:
                x3 = _attn_with_past(layer, k_past, v_past)
            xf = x3.reshape(t, d)
        outs = _token_stage(xf, wts["ffn"][i][1], ple=wts["ple"][i], p=p[i].reshape(t, -1),
                            final_g=wts["final_norm"] if i == depth - 1 else None,
                            kv=wts["kv"] if i + 1 == n_a else None)
        xf = outs[0]
        if i + 1 == n_a:
            k_new, v_new = outs[1], outs[2]
            k3 = k_new.reshape(b, l, -1)
            v3 = v_new.reshape(b, l, -1)
            if k_past is None:
                kd, vd = _kv_layout(k3, v3)
            else:
                assert k_past.shape[1] % KEY_BLOCK == 0 and l <= KEY_BLOCK
                fill = ((0, 0), (0, KEY_BLOCK - l), (0, 0))
                kd, vd = _kv_layout(jnp.pad(k3, fill), jnp.pad(v3, fill))
    kv_shape = (b, l, SB_KV_HEADS, SB_HEAD_DIM)
    return (xf.reshape(b, l, d), jnp.stack(ssm_out), jnp.stack(conv_out),
            k_new.reshape(kv_shape), v_new.reshape(kv_shape))


def kernel(x_prompt, x_sample, p_prompt, p_sample, state_ssm, state_conv, cache_k, cache_v,
           ffn_norm, ffn_w_gate, ffn_w_up, ffn_w_down, mix_norm,
           ssm_w_in, ssm_conv_w, ssm_conv_b, ssm_dt_bias, ssm_a_log, ssm_d, ssm_norm, ssm_w_out,
           kv_norm, w_k, w_v, sb_w_q, sb_w_o,
           ple_norm, ple_w_gate, ple_w_proj, final_norm):
    depth, d = mix_norm.shape
    n_a = ssm_w_in.shape[0]
    n_b = sb_w_q.shape[0]
    head_order = _pair_major_order(sb_w_q.shape[2] // SB_HEAD_DIM)
    wts = {
        "ffn_norm": ffn_norm,
        "ffn": [[{"g": ffn_norm[i, s].reshape(1, d),
                  "wg": ffn_w_gate[i, s].astype(BF16),
                  "wu": ffn_w_up[i, s].astype(BF16),
                  "wd": ffn_w_down[i, s].astype(BF16)} for s in range(2)] for i in range(depth)],
        "mix_norm": [mix_norm[i].reshape(1, d) for i in range(depth)],
        "mamba": [_prepare_mamba(ssm_w_in[i], ssm_conv_w[i], ssm_conv_b[i], ssm_dt_bias[i],
                                 ssm_a_log[i], ssm_d[i], ssm_norm[i], ssm_w_out[i])
                  for i in range(n_a)],
        "kv": {"g": kv_norm.reshape(1, d), "wk": w_k.astype(BF16), "wv": w_v.astype(BF16)},
        "wq": [sb_w_q[j][:, head_order].astype(BF16) for j in range(n_b)],
        "wo": [sb_w_o[j][head_order, :].astype(BF16) for j in range(n_b)],
        "ple": [{"g": ple_norm[i].reshape(1, d),
                 "wgate": ple_w_gate[i].astype(BF16),
                 "wproj": ple_w_proj[i].astype(BF16)} for i in range(depth)],
        "final_norm": final_norm.reshape(1, d),
    }
    y_p, ssm_p, conv_p, k_p, v_p = _trunk(x_prompt, p_prompt, None, None, None, None, wts)
    y_s, ssm_s, conv_s, k_s, v_s = _trunk(x_sample, p_sample, state_ssm, state_conv,
                                          cache_k, cache_v, wts)
    return (y_p, y_s, ssm_p, conv_p, k_p, v_p, ssm_s, conv_s, k_s, v_s)
```
